```python
import math
import jax, jax.numpy as jnp
from jax import lax
import numpy as np

D_MODEL = 1024
BATCH = 16
SEQ = 4096
DEPTH = 1
DEC_BATCH = 128
DEC_SEQ = 4
PAST_LEN = 8192
PAGE_SIZE = 128

RW_HEAD_DIM = 64
RW_WIDTH = D_MODEL // 2
RW_HEADS = RW_WIDTH // RW_HEAD_DIM
RW_DECAY_RANK = 64
RW_ICLR_RANK = 64
RW_SHIFT_WIDTH = 3 * RW_WIDTH + RW_DECAY_RANK + RW_ICLR_RANK
RW_GN_EPS = 64e-5
AT_HEAD_DIM = 64
AT_WIDTH = D_MODEL // 2
AT_HEADS = AT_WIDTH // AT_HEAD_DIM
MOBA_BLOCK = 256
MOBA_TOPK = 3
Q_CHUNK = 128
REL_BUCKETS = 32
REL_MAX_DIST = 128
NORM_EPS = 1e-6
NEG_BIG = -1e30
IN_SPLITS = (
    RW_SHIFT_WIDTH,
    RW_SHIFT_WIDTH + RW_WIDTH,
    RW_SHIFT_WIDTH + RW_WIDTH + AT_WIDTH,
    RW_SHIFT_WIDTH + RW_WIDTH + 2 * AT_WIDTH,
    RW_SHIFT_WIDTH + RW_WIDTH + 3 * AT_WIDTH,
    RW_SHIFT_WIDTH + RW_WIDTH + 4 * AT_WIDTH,
    RW_SHIFT_WIDTH + RW_WIDTH + 4 * AT_WIDTH + D_MODEL,
)
IN_WIDTH = RW_SHIFT_WIDTH + RW_WIDTH + 4 * AT_WIDTH + 2 * D_MODEL
RW_SPLITS = (RW_WIDTH, 2 * RW_WIDTH, 3 * RW_WIDTH, 3 * RW_WIDTH + RW_DECAY_RANK)

kernel_name = 'rwkv7_moba_gated_hybrid_step'


def rmsnorm(x, g):
    xf = x.astype(jnp.float32)
    y = xf * lax.rsqrt(jnp.mean(xf * xf, -1, keepdims=True) + NORM_EPS) * g.astype(jnp.float32)
    return y.astype(x.dtype)


def head_rmsnorm(t, g):
    tf = t.astype(jnp.float32)
    y = tf * lax.rsqrt(jnp.mean(tf * tf, -1, keepdims=True) + NORM_EPS) * g.astype(jnp.float32)
    return y.astype(t.dtype)


def rel_bucket(dist):
    exact = REL_BUCKETS // 2
    d = jnp.maximum(dist, 0)
    logd = jnp.log(jnp.maximum(d, 1).astype(jnp.float32) / exact) / math.log(REL_MAX_DIST / exact)
    large = jnp.minimum(exact + (logd * (REL_BUCKETS - exact)).astype(jnp.int32), REL_BUCKETS - 1)
    return jnp.where(d < exact, d, large)


def rwkv7_time_mix(z, z_last, s0, mu, w0, w_up, a0, a_up, k_k, k_a, r_k, gn_w, gn_b):
    B, T, _ = z.shape
    zf = z.astype(jnp.float32)
    z_prev = jnp.concatenate([z_last.astype(jnp.float32)[:, None], zf[:, :-1]], axis=1)
    zl = zf + (z_prev - zf) * mu
    r, k, v, wd, ad = jnp.split(zl, RW_SPLITS, axis=-1)
    w = -jax.nn.softplus(-(w0 + jnp.tanh(wd) @ w_up)) - 0.5
    decay = jnp.exp(-jnp.exp(w))
    a = jax.nn.sigmoid(a0 + ad @ a_up)
    heads = lambda t: t.reshape(B, T, RW_HEADS, RW_HEAD_DIM)
    kk = heads(k * k_k)
    kk = kk / jnp.maximum(jnp.sqrt(jnp.sum(kk * kk, -1, keepdims=True)), 1e-12)
    k = k * (1.0 + (a - 1.0) * k_a)
    r, k, v, decay, a = heads(r), heads(k), heads(v), heads(decay), heads(a)

    def step(S, inp):
        r_t, w_t, k_t, v_t, kk_t, a_t = inp
        s_kk = jnp.einsum('bhvk,bhk->bhv', S, kk_t)
        S = (S * w_t[:, :, None, :]
             - s_kk[..., None] * (kk_t * a_t)[:, :, None, :]
             + v_t[..., None] * k_t[:, :, None, :])
        return S, jnp.einsum('bhvk,bhk->bhv', S, r_t)

    xs = tuple(jnp.swapaxes(t, 0, 1) for t in (r, decay, k, v, kk, a))
    s_new, y = lax.scan(step, s0.astype(jnp.float32), xs)
    y = jnp.swapaxes(y, 0, 1)
    yc = y - jnp.mean(y, -1, keepdims=True)
    y = yc * lax.rsqrt(jnp.mean(yc * yc, -1, keepdims=True) + RW_GN_EPS)
    y = y * gn_w.reshape(RW_HEADS, RW_HEAD_DIM) + gn_b.reshape(RW_HEADS, RW_HEAD_DIM)
    y = y + jnp.sum(r * k * r_k, -1, keepdims=True) * v
    return y.reshape(B, T, RW_WIDTH), s_new, zf[:, -1]


def moba_attend(q, q_pos, keys, vals, kmean, rel_table):
    B, Q, H, dh = q.shape
    L = keys.shape[1]
    nbf = kmean.shape[1]
    own = q_pos // MOBA_BLOCK
    own_b = jnp.broadcast_to(own[None, None, :, None], (B, H, Q, 1))
    n_sel = min(MOBA_TOPK, nbf)
    if n_sel > 0:
        gate = jnp.einsum('bqhd,bnhd->bhqn', q, kmean).astype(jnp.float32)
        fully_past = jnp.arange(nbf, dtype=jnp.int32)[None, None, None, :] < own[None, None, :, None]
        gate = jnp.where(fully_past, gate, -jnp.inf)
        _, sel = lax.top_k(gate, n_sel)
        blk = jnp.concatenate([sel.astype(jnp.int32), own_b], axis=-1)
        blk_ok = jnp.concatenate([sel < own_b, jnp.ones_like(own_b, dtype=bool)], axis=-1)
    else:
        blk = own_b
        blk_ok = jnp.ones_like(own_b, dtype=bool)
    pos = blk[..., None] * MOBA_BLOCK + jnp.arange(MOBA_BLOCK, dtype=jnp.int32)
    dist = q_pos[None, None, :, None, None] - pos
    valid = blk_ok[..., None] & (dist >= 0)
    posc = jnp.clip(pos, 0, L - 1)
    bi = jnp.arange(B)[:, None, None, None, None]
    hi = jnp.arange(H)[None, :, None, None, None]
    kg = keys[bi, posc, hi]
    vg = vals[bi, posc, hi]
    logits = jnp.einsum('bqhd,bhqsld->bhqsl', q, kg).astype(jnp.float32) * (dh ** -0.5)
    logits = logits + rel_table[rel_bucket(dist), hi].astype(jnp.float32)
    logits = jnp.where(valid, logits, NEG_BIG)
    S = blk.shape[-1]
    p = jax.nn.softmax(logits.reshape(B, H, Q, S * MOBA_BLOCK), axis=-1)
    return jnp.einsum('bhqm,bhqmd->bqhd', p.astype(vg.dtype), vg.reshape(B, H, Q, S * MOBA_BLOCK, dh))


def moba_attention(q, pos0, keys, vals, rel_table):
    B, T, H, dh = q.shape
    L = keys.shape[1]
    nbf = L // MOBA_BLOCK
    kmean = keys[:, :nbf * MOBA_BLOCK].reshape(B, nbf, MOBA_BLOCK, H, dh).astype(jnp.float32).mean(2)
    q_pos = pos0 + jnp.arange(T, dtype=jnp.int32)
    if T <= Q_CHUNK:
        return moba_attend(q, q_pos, keys, vals, kmean, rel_table)
    nc = T // Q_CHUNK
    qc = jnp.swapaxes(q.reshape(B, nc, Q_CHUNK, H, dh), 0, 1)
    pc = q_pos.reshape(nc, Q_CHUNK)
    out = lax.map(lambda a: moba_attend(a[0], a[1], keys, vals, kmean, rel_table), (qc, pc))
    return jnp.swapaxes(out, 0, 1).reshape(B, T, H, dh)


def hybrid_layer(x, pos0, shift_prev, wkv_prev, past_k, past_v, rel_table, p):
    (norm_g, w_in, rw_mu, rw_w0, rw_w_up, rw_a0, rw_a_up, rw_k_k, rw_k_a, rw_r_k,
     rw_gn_w, rw_gn_b, at_q_norm, at_k_norm, w_o_rwkv, w_o_attn, w_out) = p
    B, T, _ = x.shape
    h = rmsnorm(x, norm_g)
    z = jnp.einsum('btd,de->bte', h, w_in)
    z_rw, g_rw, q, k, v, g_at, m_rw, m_at = jnp.split(z, IN_SPLITS, axis=-1)
    y_rw, wkv_new, shift_new = rwkv7_time_mix(z_rw, shift_prev, wkv_prev, rw_mu, rw_w0, rw_w_up,
                                              rw_a0, rw_a_up, rw_k_k, rw_k_a, rw_r_k, rw_gn_w, rw_gn_b)
    q = head_rmsnorm(q.reshape(B, T, AT_HEADS, AT_HEAD_DIM), at_q_norm)
    k = head_rmsnorm(k.reshape(B, T, AT_HEADS, AT_HEAD_DIM), at_k_norm)
    v = v.reshape(B, T, AT_HEADS, AT_HEAD_DIM)
    if past_k is None:
        keys, vals = k, v
    else:
        keys = jnp.concatenate([past_k.astype(k.dtype), k], axis=1)
        vals = jnp.concatenate([past_v.astype(v.dtype), v], axis=1)
    y_at = moba_attention(q, pos0, keys, vals, rel_table).reshape(B, T, AT_WIDTH)
    o_rw = jnp.einsum('btc,cd->btd', y_rw * jax.nn.silu(g_rw), w_o_rwkv)
    o_at = jnp.einsum('btc,cd->btd', y_at * jax.nn.silu(g_at), w_o_attn)
    merged = jax.nn.sigmoid(m_rw) * o_rw + jax.nn.sigmoid(m_at) * o_at
    y = x + jnp.einsum('btd,de->bte', merged, w_out)
    return y.astype(x.dtype), k, v, shift_new, wkv_new


def setup_inputs(seed: int = 0) -> dict:
    key = jax.random.key(seed)
    ks = jax.random.split(key, 32)
    f32 = jnp.float32
    nrm = lambda k, shape, s: s * jax.random.normal(k, shape, f32)
    n_pages = PAST_LEN // PAGE_SIZE
    n_pool = (DEC_BATCH * n_pages * 5) // 4
    ramp = jnp.linspace(0.0, 1.0, RW_WIDTH, dtype=f32) ** 0.85
    page_table = jax.random.permutation(ks[4], n_pool)[:DEC_BATCH * n_pages]
    page_table = page_table.reshape(DEC_BATCH, n_pages).astype(jnp.int32)
    return {
        'x_prompt': nrm(ks[0], (BATCH, SEQ, D_MODEL), 1.0),
        'x_sample': nrm(ks[1], (DEC_BATCH, DEC_SEQ, D_MODEL), 1.0),
        'cache_k': nrm(ks[2], (DEPTH, n_pool, PAGE_SIZE, AT_HEADS, AT_HEAD_DIM), 1.0),
        'cache_v': nrm(ks[3], (DEPTH, n_pool, PAGE_SIZE, AT_HEADS, AT_HEAD_DIM), 1.0),
        'page_table': page_table,
        'state_shift': nrm(ks[5], (DEPTH, DEC_BATCH, RW_SHIFT_WIDTH), 1.0),
        'state_wkv': nrm(ks[6], (DEPTH, DEC_BATCH, RW_HEADS, RW_HEAD_DIM, RW_HEAD_DIM), 0.3),
        'rel_table': nrm(ks[7], (REL_BUCKETS, AT_HEADS), 0.1),
        'norm_g': 1.0 + nrm(ks[8], (DEPTH, D_MODEL), 0.05),
        'w_in': nrm(ks[9], (DEPTH, D_MODEL, IN_WIDTH), D_MODEL ** -0.5),
        'rw_mu': jax.random.uniform(ks[10], (DEPTH, RW_SHIFT_WIDTH), f32),
        'rw_w0': (-6.5 + 5.0 * ramp)[None] + nrm(ks[11], (DEPTH, RW_WIDTH), 0.1),
        'rw_w_up': nrm(ks[12], (DEPTH, RW_DECAY_RANK, RW_WIDTH), 0.5 * RW_DECAY_RANK ** -0.5),
        'rw_a0': nrm(ks[13], (DEPTH, RW_WIDTH), 0.1),
        'rw_a_up': nrm(ks[14], (DEPTH, RW_ICLR_RANK, RW_WIDTH), RW_ICLR_RANK ** -0.5),
        'rw_k_k': 0.85 + nrm(ks[15], (DEPTH, RW_WIDTH), 0.05),
        'rw_k_a': 1.0 + nrm(ks[16], (DEPTH, RW_WIDTH), 0.05),
        'rw_r_k': nrm(ks[17], (DEPTH, RW_HEADS, RW_HEAD_DIM), 0.1),
        'rw_gn_w': 1.0 + nrm(ks[18], (DEPTH, RW_WIDTH), 0.05),
        'rw_gn_b': nrm(ks[19], (DEPTH, RW_WIDTH), 0.01),
        'at_q_norm': 1.0 + nrm(ks[20], (DEPTH, AT_HEAD_DIM), 0.05),
        'at_k_norm': 1.0 + nrm(ks[21], (DEPTH, AT_HEAD_DIM), 0.05),
        'w_o_rwkv': nrm(ks[22], (DEPTH, RW_WIDTH, D_MODEL), RW_WIDTH ** -0.5),
        'w_o_attn': nrm(ks[23], (DEPTH, AT_WIDTH, D_MODEL), AT_WIDTH ** -0.5),
        'w_out': nrm(ks[24], (DEPTH, D_MODEL, D_MODEL), D_MODEL ** -0.5),
    }


def reference(x_prompt, x_sample, cache_k, cache_v, page_table, state_shift, state_wkv, rel_table,
              norm_g, w_in, rw_mu, rw_w0, rw_w_up, rw_a0, rw_a_up, rw_k_k, rw_k_a, rw_r_k,
              rw_gn_w, rw_gn_b, at_q_norm, at_k_norm, w_o_rwkv, w_o_attn, w_out):
    n_prompt = x_prompt.shape[0]
    n_dec = page_table.shape[0]
    y_p, y_s = x_prompt, x_sample
    kp_l, vp_l, sp_l, wp_l = [], [], [], []
    ks_l, vs_l, ss_l, ws_l = [], [], [], []
    for l in range(DEPTH):
        p = (norm_g[l], w_in[l], rw_mu[l], rw_w0[l], rw_w_up[l], rw_a0[l], rw_a_up[l], rw_k_k[l],
             rw_k_a[l], rw_r_k[l], rw_gn_w[l], rw_gn_b[l], at_q_norm[l], at_k_norm[l],
             w_o_rwkv[l], w_o_attn[l], w_out[l])
        zero_shift = jnp.zeros((n_prompt, RW_SHIFT_WIDTH), jnp.float32)
        zero_wkv = jnp.zeros((n_prompt, RW_HEADS, RW_HEAD_DIM, RW_HEAD_DIM), jnp.float32)
        y_p, k_new, v_new, sh_new, wkv_new = hybrid_layer(y_p, 0, zero_shift, zero_wkv, None, None,
                                                          rel_table, p)
        kp_l.append(k_new); vp_l.append(v_new); sp_l.append(sh_new); wp_l.append(wkv_new)
        past_k = cache_k[l][page_table].reshape(n_dec, PAST_LEN, AT_HEADS, AT_HEAD_DIM)
        past_v = cache_v[l][page_table].reshape(n_dec, PAST_LEN, AT_HEADS, AT_HEAD_DIM)
        y_s, k_new, v_new, sh_new, wkv_new = hybrid_layer(y_s, PAST_LEN, state_shift[l], state_wkv[l],
                                                          past_k, past_v, rel_table, p)
        ks_l.append(k_new); vs_l.append(v_new); ss_l.append(sh_new); ws_l.append(wkv_new)
    return (y_p, y_s, jnp.stack(kp_l), jnp.stack(vp_l), jnp.stack(sp_l), jnp.stack(wp_l),
            jnp.stack(ks_l), jnp.stack(vs_l), jnp.stack(ss_l), jnp.stack(ws_l))
```

```python
import functools
import math

import jax
import jax.numpy as jnp
from jax import lax
from jax.experimental import pallas as pl
from jax.experimental.pallas import tpu as pltpu

D_MODEL = 1024
HEAD_DIM = 64
RW_WIDTH = 512
RW_HEADS = 8
RW_RANK = 64
RW_SHIFT_WIDTH = 3 * RW_WIDTH + 2 * RW_RANK
RW_GN_EPS = 64e-5
AT_WIDTH = 512
AT_HEADS = 8
MOBA_BLOCK = 256
MOBA_TOPK = 3
PAGE_SIZE = 128
REL_BUCKETS = 32
REL_MAX_DIST = 128
NORM_EPS = 1e-6
NEG_BIG = -1e30
IN_WIDTH = RW_SHIFT_WIDTH + RW_WIDTH + 4 * AT_WIDTH + 2 * D_MODEL
C_GRW = RW_SHIFT_WIDTH
C_Q = C_GRW + RW_WIDTH
C_K = C_Q + AT_WIDTH
C_V = C_K + AT_WIDTH
C_GAT = C_V + AT_WIDTH
C_M = C_GAT + AT_WIDTH

VMEM_LIMIT_V7X = 56 * 1024 * 1024
F32 = jnp.float32
BF16 = jnp.bfloat16
HIGHEST = lax.Precision.HIGHEST

NN = (((1,), (0,)), ((), ()))
NT = (((1,), (1,)), ((), ()))
TN = (((0,), (0,)), ((), ()))


def _dot(a, b, dims=NN, precision=None):
    return lax.dot_general(a, b, dims, precision=precision, preferred_element_type=F32)


def _split_dot(x, ones_bf16, passes):
    acc = None
    rem = x
    for _ in range(passes):
        part = rem.astype(BF16)
        rem = rem - part.astype(F32)
        term = _dot(part, ones_bf16)
        acc = term if acc is None else acc + term
    return acc


def _sigmoid(x):
    return 1.0 / (1.0 + jnp.exp(-x))


def _silu(x):
    return x * _sigmoid(x)


def _head_ones():
    r = lax.broadcasted_iota(jnp.int32, (RW_WIDTH, RW_WIDTH), 0) // HEAD_DIM
    c = lax.broadcasted_iota(jnp.int32, (RW_WIDTH, RW_WIDTH), 1) // HEAD_DIM
    return (r == c).astype(BF16)


def _in_proj_kernel(x_ref, g_ref, w_ref, qn_ref, kn_ref,
                    zrw_ref, grw_ref, gat_ref, m_ref, k_ref, v_ref, qb_ref, kb_ref, vb_ref):
    x = x_ref[...]
    ms = jnp.mean(x * x, axis=-1, keepdims=True)
    h = (x * lax.rsqrt(ms + NORM_EPS) * g_ref[...]).astype(BF16)
    ones = _head_ones()

    def proj(lo, hi):
        return _dot(h, w_ref[:, lo:hi])

    def head_norm(z, gain):
        msq = _split_dot(z * z, ones, 3) * (1.0 / HEAD_DIM)
        return z * lax.rsqrt(msq + NORM_EPS) * gain

    zrw_ref[...] = proj(0, C_GRW)
    grw_ref[...] = proj(C_GRW, C_Q)
    q = head_norm(proj(C_Q, C_K), qn_ref[...])
    qb_ref[...] = q.astype(BF16)
    k = head_norm(proj(C_K, C_V), kn_ref[...])
    k_ref[...] = k
    kb_ref[...] = k.astype(BF16)
    v = proj(C_V, C_GAT)
    v_ref[...] = v
    vb_ref[...] = v.astype(BF16)
    gat_ref[...] = proj(C_GAT, C_M)
    m_ref[...] = proj(C_M, IN_WIDTH)


def _in_proj(x2d, norm_g, w_bf16, q_norm, k_norm, tm):
    n = x2d.shape[0]
    row = lambda w: pl.BlockSpec((tm, w), lambda i: (i, 0))
    full = lambda a: pl.BlockSpec(a.shape, lambda i: (0,) * a.ndim)
    g2 = norm_g.reshape(1, D_MODEL)
    qn = jnp.tile(q_norm, AT_HEADS).reshape(1, AT_WIDTH)
    kn = jnp.tile(k_norm, AT_HEADS).reshape(1, AT_WIDTH)
    widths = (RW_SHIFT_WIDTH, RW_WIDTH, AT_WIDTH, 2 * D_MODEL, AT_WIDTH, AT_WIDTH)
    out_shape = [jax.ShapeDtypeStruct((n, w), F32) for w in widths]
    out_shape += [jax.ShapeDtypeStruct((n, AT_WIDTH), BF16)] * 3
    out_specs = [row(w) for w in widths] + [row(AT_WIDTH)] * 3
    return pl.pallas_call(
        _in_proj_kernel,
        grid=(n // tm,),
        in_specs=[row(D_MODEL), full(g2), full(w_bf16), full(qn), full(kn)],
        out_specs=out_specs,
        out_shape=out_shape,
        compiler_params=pltpu.CompilerParams(dimension_semantics=("parallel",),
                                             vmem_limit_bytes=VMEM_LIMIT_V7X),
        name="in_proj",
    )(x2d, g2, w_bf16, qn, kn)


def _rwkv_kernel(z_ref, zlast_ref, s0_ref, mu_ref, w0_ref, wup_ref, a0_ref, aup_ref, kk_ref, ka_ref,
                 rk_ref, gnw_ref, gnb_ref, y_ref, sout_ref, s_scr, zprev_scr, *, chunk, n_valid):
    c = pl.program_id(1)

    @pl.when(c == 0)
    def _():
        s_scr[...] = s0_ref[0]
        zprev_scr[...] = zlast_ref[0]

    z = z_ref[0]
    row = lax.broadcasted_iota(jnp.int32, (chunk, 1), 0)
    zp = jnp.where(row == 0, zprev_scr[...], pltpu.roll(z, 1, axis=0))
    zprev_scr[...] = z[n_valid - 1:n_valid]
    zl = z + (zp - z) * mu_ref[...]
    r = zl[:, 0:RW_WIDTH]
    k = zl[:, RW_WIDTH:2 * RW_WIDTH]
    v = zl[:, 2 * RW_WIDTH:3 * RW_WIDTH]
    wd = zl[:, 3 * RW_WIDTH:3 * RW_WIDTH + RW_RANK]
    ad = zl[:, 3 * RW_WIDTH + RW_RANK:RW_SHIFT_WIDTH]

    x = -(w0_ref[...] + _dot(jnp.tanh(wd).astype(BF16), wup_ref[...]))
    softplus = jnp.maximum(x, 0.0) + jnp.log(1.0 + jnp.exp(-jnp.abs(x)))
    w = -softplus - 0.5
    ld = -jnp.exp(w)
    a = _sigmoid(a0_ref[...] + _dot(ad.astype(BF16), aup_ref[...]))
    ones = _head_ones()
    kk = k * kk_ref[...]
    kk = kk / jnp.maximum(jnp.sqrt(_split_dot(kk * kk, ones, 3)), 1e-12)
    kmod = k * (1.0 + (a - 1.0) * ka_ref[...])
    b = kk * a
    if n_valid < chunk:
        ok = row < n_valid
        ld = jnp.where(ok, ld, 0.0)
        kk = jnp.where(ok, kk, 0.0)
        kmod = jnp.where(ok, kmod, 0.0)
        b = jnp.where(ok, b, 0.0)

    ti = lax.broadcasted_iota(jnp.int32, (chunk, chunk), 0)
    tj = lax.broadcasted_iota(jnp.int32, (chunk, chunk), 1)
    incl = ti >= tj
    strict = ti > tj
    eye = (ti == tj).astype(F32)
    cum = _dot(incl.astype(F32), ld, precision=HIGHEST)
    tot = cum[chunk - 1:chunk]
    p = jnp.exp(cum)
    kkt = kk * jnp.exp(cum - ld)
    rt = r * p
    pinv = jnp.exp(-cum)
    kh = kmod * pinv
    bh = b * pinv
    pend = jnp.exp(tot - cum)
    kp = kmod * pend
    bp = b * pend
    ptot = jnp.exp(tot)

    ys = []
    for hd in range(RW_HEADS):
        sl = slice(hd * HEAD_DIM, (hd + 1) * HEAD_DIM)
        v_h = v[:, sl]
        lhs = jnp.concatenate([kkt[:, sl], rt[:, sl]], axis=0)
        rhs = jnp.concatenate([kh[:, sl], bh[:, sl]], axis=0)
        g = _dot(lhs, rhs, NT, HIGHEST)
        a_k = jnp.where(strict, g[:chunk, :chunk], 0.0)
        a_b = jnp.where(strict, g[:chunk, chunk:], 0.0)
        a_rk = jnp.where(incl, g[chunk:, :chunk], 0.0)
        a_rb = jnp.where(incl, g[chunk:, chunk:], 0.0)
        pw = -a_b
        tinv = eye + pw
        for _ in range(int(math.log2(chunk)) - 1):
            pw = _dot(pw, pw, precision=HIGHEST)
            tinv = tinv + _dot(tinv, pw, precision=HIGHEST)
        s0 = s_scr[hd]
        rhs_u = _dot(kkt[:, sl], s0, NT, HIGHEST) + _dot(a_k, v_h, precision=HIGHEST)
        u = _dot(tinv, rhs_u, precision=HIGHEST)
        y = (_dot(rt[:, sl], s0, NT, HIGHEST) + _dot(a_rk, v_h, precision=HIGHEST)
             - _dot(a_rb, u, precision=HIGHEST))
        s_scr[hd] = (s0 * ptot[:, sl] + _dot(v_h, kp[:, sl], TN, HIGHEST)
                     - _dot(u, bp[:, sl], TN, HIGHEST))
        ys.append(y)
    y = jnp.concatenate(ys, axis=1)

    inv_n = 1.0 / HEAD_DIM
    yc = y - _split_dot(y, ones, 3) * inv_n
    var = _split_dot(yc * yc, ones, 3) * inv_n
    y = yc * lax.rsqrt(var + RW_GN_EPS) * gnw_ref[...] + gnb_ref[...]
    y = y + _split_dot(r * kmod * rk_ref[...], ones, 3) * v
    y_ref[0] = y

    @pl.when(c == pl.num_programs(1) - 1)
    def _():
        sout_ref[0] = s_scr[...]


def _rwkv(z3, zlast, s0, params, chunk, n_valid):
    bsz, t, _ = z3.shape
    nc = t // chunk
    full = lambda a: pl.BlockSpec(a.shape, lambda bi, ci: (0,) * a.ndim)
    vecs = [p.reshape(1, -1) for p in params]
    mu, w0, wup, a0, aup, k_k, k_a, r_k, gn_w, gn_b = params
    ops = [mu.reshape(1, -1), w0.reshape(1, -1), wup.astype(BF16), a0.reshape(1, -1), aup.astype(BF16),
           k_k.reshape(1, -1), k_a.reshape(1, -1), r_k.reshape(1, -1), gn_w.reshape(1, -1), gn_b.reshape(1, -1)]
    del vecs
    kern = functools.partial(_rwkv_kernel, chunk=chunk, n_valid=n_valid)
    return pl.pallas_call(
        kern,
        grid=(bsz, nc),
        in_specs=[pl.BlockSpec((1, chunk, RW_SHIFT_WIDTH), lambda bi, ci: (bi, ci, 0)),
                  pl.BlockSpec((1, 1, RW_SHIFT_WIDTH), lambda bi, ci: (bi, 0, 0)),
                  pl.BlockSpec((1, RW_HEADS, HEAD_DIM, HEAD_DIM), lambda bi, ci: (bi, 0, 0, 0))]
                 + [full(o) for o in ops],
        out_specs=[pl.BlockSpec((1, chunk, RW_WIDTH), lambda bi, ci: (bi, ci, 0)),
                   pl.BlockSpec((1, RW_HEADS, HEAD_DIM, HEAD_DIM), lambda bi, ci: (bi, 0, 0, 0))],
        out_shape=[jax.ShapeDtypeStruct((bsz, t, RW_WIDTH), F32),
                   jax.ShapeDtypeStruct((bsz, RW_HEADS, HEAD_DIM, HEAD_DIM), F32)],
        scratch_shapes=[pltpu.VMEM((RW_HEADS, HEAD_DIM, HEAD_DIM), F32),
                        pltpu.VMEM((1, RW_SHIFT_WIDTH), F32)],
        compiler_params=pltpu.CompilerParams(dimension_semantics=("parallel", "arbitrary"),
                                             vmem_limit_bytes=VMEM_LIMIT_V7X),
        name="rwkv",
    )(z3, zlast.reshape(bsz, 1, RW_SHIFT_WIDTH), s0, *ops)


def _bias_kernel(rel_ref, o_ref):
    hd = pl.program_id(0)
    kj = lax.broadcasted_iota(jnp.int32, (MOBA_BLOCK, MOBA_BLOCK), 0)
    qi = lax.broadcasted_iota(jnp.int32, (MOBA_BLOCK, MOBA_BLOCK), 1)
    exact = REL_BUCKETS // 2
    for which in range(2):
        d = jnp.maximum(qi - kj + which * MOBA_BLOCK, 0)
        logd = jnp.log(jnp.maximum(d, 1).astype(F32) / exact) / math.log(REL_MAX_DIST / exact)
        large = jnp.minimum(exact + (logd * (REL_BUCKETS - exact)).astype(jnp.int32), REL_BUCKETS - 1)
        bucket = jnp.where(d < exact, d, large)
        acc = jnp.zeros((MOBA_BLOCK, MOBA_BLOCK), F32)
        for bkt in range(REL_BUCKETS):
            acc = jnp.where(bucket == bkt, rel_ref[bkt, hd], acc)
        o_ref[0, which] = acc


def _bias_tables(rel_table):
    return pl.pallas_call(
        _bias_kernel,
        grid=(AT_HEADS,),
        in_specs=[pl.BlockSpec(memory_space=pltpu.SMEM)],
        out_specs=pl.BlockSpec((1, 2, MOBA_BLOCK, MOBA_BLOCK), lambda h: (h, 0, 0, 0)),
        out_shape=jax.ShapeDtypeStruct((AT_HEADS, 2, MOBA_BLOCK, MOBA_BLOCK), F32),
        name="rel_bias",
    )(rel_table)


def _topk_mask(gate, valid, blk, nblk):
    rank = jnp.zeros(gate.shape, jnp.int32)
    for jp in range(nblk):
        gj = gate[jp:jp + 1, :]
        beats = (gj > gate) | ((gj == gate) & (jp < blk))
        rank = rank + beats.astype(jnp.int32)
    return valid & (rank < MOBA_TOPK)


def _moba_kernel(far_ref, q_ref, k_ref, v_ref, bias_ref, o_ref, kmean_scr, sel_scr, *, nblk):
    hp = pl.program_id(1)
    i = pl.program_id(2)

    @pl.when(i == 0)
    def _():
        kf = k_ref[0].astype(F32).reshape(nblk, MOBA_BLOCK, 2 * HEAD_DIM)
        kmean_scr[...] = jnp.sum(kf, axis=1) * (1.0 / MOBA_BLOCK)

    kj = lax.broadcasted_iota(jnp.int32, (MOBA_BLOCK, MOBA_BLOCK), 0)
    qi = lax.broadcasted_iota(jnp.int32, (MOBA_BLOCK, MOBA_BLOCK), 1)
    causal = kj <= qi
    blk = lax.broadcasted_iota(jnp.int32, (nblk, MOBA_BLOCK), 0)
    q2 = q_ref[0]
    own0 = pl.multiple_of(i * MOBA_BLOCK, MOBA_BLOCK)
    prev_blk = jnp.maximum(i - 1, 0)
    prev0 = pl.multiple_of(prev_blk * MOBA_BLOCK, MOBA_BLOCK)
    outs = []
    for hh in range(2):
        sl = slice(hh * HEAD_DIM, (hh + 1) * HEAD_DIM)
        q = q2[:, sl]
        gate = _dot(kmean_scr[:, sl], q.astype(F32), NT, HIGHEST)
        valid = blk < i
        gate = jnp.where(valid, gate, -jnp.inf)
        sel_scr[hh] = _topk_mask(gate, valid, blk, nblk).astype(F32)
        qs = q * 0.125

        def block(j0, bias, mask, carry):
            m, l, acc = carry
            kb = k_ref[0, pl.ds(j0, MOBA_BLOCK), sl]
            vb = v_ref[0, pl.ds(j0, MOBA_BLOCK), sl]
            s = jnp.where(mask, _dot(kb, qs, NT) + bias, NEG_BIG)
            m_new = jnp.maximum(m, jnp.max(s, axis=0, keepdims=True))
            alpha = jnp.exp(m - m_new)
            pr = jnp.exp(s - m_new)
            l = alpha * l + jnp.sum(pr, axis=0, keepdims=True)
            acc = alpha * acc + _dot(vb, pr.astype(BF16), TN)
            return m_new, l, acc

        carry = (jnp.full((1, MOBA_BLOCK), NEG_BIG, F32), jnp.zeros((1, MOBA_BLOCK), F32),
                 jnp.zeros((HEAD_DIM, MOBA_BLOCK), F32))
        carry = block(own0, bias_ref[hh, 0], causal, carry)
        carry = block(prev0, bias_ref[hh, 1], sel_scr[hh, pl.ds(prev_blk, 1), :] > 0.5, carry)
        far = far_ref[2 * hp + hh]

        def far_body(j, carry):
            j0 = pl.multiple_of(j * MOBA_BLOCK, MOBA_BLOCK)
            return block(j0, far, sel_scr[hh, pl.ds(j, 1), :] > 0.5, carry)

        m, l, acc = lax.fori_loop(0, i - 1, far_body, carry)
        outs.append(jnp.transpose(acc / l))
    o_ref[0] = jnp.concatenate(outs, axis=1)


def _moba_prompt(qb, kb, vb, bias, far):
    bsz, t, _ = qb.shape
    nblk = t // MOBA_BLOCK
    kern = functools.partial(_moba_kernel, nblk=nblk)
    return pl.pallas_call(
        kern,
        grid=(bsz, AT_HEADS // 2, nblk),
        in_specs=[pl.BlockSpec(memory_space=pltpu.SMEM),
                  pl.BlockSpec((1, MOBA_BLOCK, 2 * HEAD_DIM), lambda b, h, i: (b, i, h)),
                  pl.BlockSpec((1, t, 2 * HEAD_DIM), lambda b, h, i: (b, 0, h)),
                  pl.BlockSpec((1, t, 2 * HEAD_DIM), lambda b, h, i: (b, 0, h)),
                  pl.BlockSpec((2, 2, MOBA_BLOCK, MOBA_BLOCK), lambda b, h, i: (h, 0, 0, 0))],
        out_specs=pl.BlockSpec((1, MOBA_BLOCK, 2 * HEAD_DIM), lambda b, h, i: (b, i, h)),
        out_shape=jax.ShapeDtypeStruct((bsz, t, AT_WIDTH), F32),
        scratch_shapes=[pltpu.VMEM((nblk, 2 * HEAD_DIM), F32),
                        pltpu.VMEM((2, nblk, MOBA_BLOCK), F32)],
        compiler_params=pltpu.CompilerParams(dimension_semantics=("parallel", "parallel", "arbitrary"),
                                             vmem_limit_bytes=VMEM_LIMIT_V7X),
        name="moba_prompt",
    )(far, qb, kb, vb, bias)


DEC_Q = 4
DEC_COLS = DEC_Q * AT_HEADS
DEC_PAD = 8


def _moba_decode_kernel(pt_ref, q_ref, knew_ref, vnew_ref, k0_ref, k1_ref, v0_ref, v1_ref,
                        blast_ref, bown_ref, far_ref, o_ref,
                        qbd_scr, gate_scr, m_scr, l_scr, acc_scr, *, nblk):
    del pt_ref
    j = pl.program_id(1)
    nrow = nblk + DEC_PAD
    lane_head = lax.broadcasted_iota(jnp.int32, (DEC_COLS, AT_WIDTH), 1) // HEAD_DIM
    row_head = lax.broadcasted_iota(jnp.int32, (DEC_COLS, AT_WIDTH), 0) % AT_HEADS
    diag = lane_head == row_head

    @pl.when(j == 0)
    def _():
        q = q_ref[0].astype(F32)
        rows = [jnp.broadcast_to(q[qq:qq + 1], (AT_HEADS, AT_WIDTH)) for qq in range(DEC_Q)]
        qbd_scr[...] = jnp.where(diag, jnp.concatenate(rows, axis=0), 0.0)
        gate_scr[...] = jnp.full((nrow, DEC_COLS), -jnp.inf, F32)
        m_scr[...] = jnp.full((nrow, DEC_COLS), NEG_BIG, F32)
        l_scr[...] = jnp.zeros((nrow, DEC_COLS), F32)

    qbd = qbd_scr[...]
    qs = (qbd * 0.125).astype(BF16)

    def partial_softmax(kmat, vmat, bias, mask):
        s = _dot(kmat.astype(BF16), qs, NT) + bias
        if mask is not None:
            s = jnp.where(mask, s, NEG_BIG)
        m = jnp.max(s, axis=0, keepdims=True)
        pr = jnp.exp(s - m)
        l = jnp.sum(pr, axis=0, keepdims=True)
        acc = _dot(pr.astype(BF16), vmat.astype(BF16), TN)
        return m, l, acc

    kmat = jnp.concatenate([k0_ref[0], k1_ref[0]], axis=0)
    vmat = jnp.concatenate([v0_ref[0], v1_ref[0]], axis=0)
    kmean = jnp.sum(kmat, axis=0, keepdims=True) * (1.0 / MOBA_BLOCK)
    gate = _dot(jnp.broadcast_to(kmean, (8, AT_WIDTH)), qbd, NT, HIGHEST)
    gate_scr[pl.ds(j, 1), :] = gate[0:1]
    bias = jnp.where(j == nblk - 1, blast_ref[...], far_ref[...])
    m, l, acc = partial_softmax(kmat, vmat, bias, None)
    m_scr[pl.ds(j, 1), :] = m
    l_scr[pl.ds(j, 1), :] = l
    acc_scr[j] = acc

    @pl.when(j == nblk - 1)
    def _():
        kj = lax.broadcasted_iota(jnp.int32, (DEC_PAD, DEC_COLS), 0)
        qcol = lax.broadcasted_iota(jnp.int32, (DEC_PAD, DEC_COLS), 1) // AT_HEADS
        m_o, l_o, acc_o = partial_softmax(knew_ref[0], vnew_ref[0], bown_ref[...], kj <= qcol)
        m_scr[nblk:nblk + 1, :] = m_o
        l_scr[nblk:nblk + 1, :] = l_o
        acc_scr[nblk] = acc_o
        blk = lax.broadcasted_iota(jnp.int32, (nrow, DEC_COLS), 0)
        gates = gate_scr[...]
        valid = blk < nblk
        sel = _topk_mask(gates, valid, blk, nblk) | (blk == nblk)
        m_all = m_scr[...]
        m_top = jnp.max(jnp.where(sel, m_all, NEG_BIG), axis=0, keepdims=True)
        wgt = jnp.where(sel, jnp.exp(m_all - m_top), 0.0)
        wl = wgt * l_scr[...]
        wgt_t = jnp.transpose(wgt)
        denom = jnp.sum(jnp.transpose(wl), axis=1, keepdims=True)
        total = jnp.zeros((DEC_COLS, AT_WIDTH), F32)
        for jj in range(nblk + 1):
            total = total + wgt_t[:, jj:jj + 1] * acc_scr[jj]
        out = jnp.where(diag, total / denom, 0.0)
        o_ref[0] = jnp.sum(out.reshape(DEC_Q, AT_HEADS, AT_WIDTH), axis=1)


def _moba_decode(page_table, qb, k_new, v_new, cache_k, cache_v, blast, bown, far32):
    bsz = qb.shape[0]
    n_pages = page_table.shape[1]
    nblk = n_pages * PAGE_SIZE // MOBA_BLOCK
    per_blk = MOBA_BLOCK // PAGE_SIZE
    assert per_blk == 2
    ck = cache_k.reshape(cache_k.shape[0], PAGE_SIZE, AT_WIDTH)
    cv = cache_v.reshape(cache_v.shape[0], PAGE_SIZE, AT_WIDTH)
    page = lambda off: pl.BlockSpec((1, PAGE_SIZE, AT_WIDTH), lambda b, j, pt: (pt[b, 2 * j + off], 0, 0))
    per_b = lambda rows: pl.BlockSpec((1, rows, AT_WIDTH), lambda b, j, pt: (b, 0, 0))
    full = lambda a: pl.BlockSpec(a.shape, lambda b, j, pt: (0,) * a.ndim)
    nrow = nblk + DEC_PAD
    kern = functools.partial(_moba_decode_kernel, nblk=nblk)
    grid_spec = pltpu.PrefetchScalarGridSpec(
        num_scalar_prefetch=1,
        grid=(bsz, nblk),
        in_specs=[per_b(DEC_Q), per_b(DEC_PAD), per_b(DEC_PAD), page(0), page(1), page(0), page(1),
                  full(blast), full(bown), full(far32)],
        out_specs=per_b(DEC_Q),
        scratch_shapes=[pltpu.VMEM((DEC_COLS, AT_WIDTH), F32),
                        pltpu.VMEM((nrow, DEC_COLS), F32),
                        pltpu.VMEM((nrow, DEC_COLS), F32),
                        pltpu.VMEM((nrow, DEC_COLS), F32),
                        pltpu.VMEM((nblk + 1, DEC_COLS, AT_WIDTH), F32)],
    )
    return pl.pallas_call(
        kern,
        grid_spec=grid_spec,
        out_shape=jax.ShapeDtypeStruct((bsz, DEC_Q, AT_WIDTH), F32),
        compiler_params=pltpu.CompilerParams(dimension_semantics=("parallel", "arbitrary"),
                                             vmem_limit_bytes=VMEM_LIMIT_V7X),
        name="moba_decode",
    )(page_table, qb, k_new, v_new, ck, ck, cv, cv, blast, bown, far32)


def _out_proj_kernel(x_ref, yrw_ref, grw_ref, yat_ref, gat_ref, m_ref, worw_ref, woat_ref, wout_ref, o_ref):
    o_rw = _dot((yrw_ref[...] * _silu(grw_ref[...])).astype(BF16), worw_ref[...])
    o_at = _dot((yat_ref[...] * _silu(gat_ref[...])).astype(BF16), woat_ref[...])
    m = m_ref[...]
    merged = _sigmoid(m[:, :D_MODEL]) * o_rw + _sigmoid(m[:, D_MODEL:]) * o_at
    o_ref[...] = x_ref[...] + _dot(merged.astype(BF16), wout_ref[...])


def _out_proj(x2d, y_rw, g_rw, y_at, g_at, m, w_o_rwkv, w_o_attn, w_out, tm):
    n = x2d.shape[0]
    row = lambda w: pl.BlockSpec((tm, w), lambda i: (i, 0))
    full = lambda a: pl.BlockSpec(a.shape, lambda i: (0,) * a.ndim)
    ws = [w_o_rwkv.astype(BF16), w_o_attn.astype(BF16), w_out.astype(BF16)]
    return pl.pallas_call(
        _out_proj_kernel,
        grid=(n // tm,),
        in_specs=[row(D_MODEL), row(RW_WIDTH), row(RW_WIDTH), row(AT_WIDTH), row(AT_WIDTH), row(2 * D_MODEL)]
                 + [full(w) for w in ws],
        out_specs=row(D_MODEL),
        out_shape=jax.ShapeDtypeStruct((n, D_MODEL), F32),
        compiler_params=pltpu.CompilerParams(dimension_semantics=("parallel",),
                                             vmem_limit_bytes=VMEM_LIMIT_V7X),
        name="out_proj",
    )(x2d, y_rw, g_rw, y_at, g_at, m, *ws)


def _row_tile(n):
    return 256 if n % 256 == 0 else n


def kernel(x_prompt, x_sample, cache_k, cache_v, page_table, state_shift, state_wkv, rel_table, norm_g, w_in, rw_mu, rw_w0, rw_w_up, rw_a0, rw_a_up, rw_k_k, rw_k_a, rw_r_k, rw_gn_w, rw_gn_b, at_q_norm, at_k_norm, w_o_rwkv, w_o_attn, w_out):
    depth = w_in.shape[0]
    assert depth == 1
    l = 0
    bp, tp, _ = x_prompt.shape
    bs, ts, _ = x_sample.shape
    assert ts == DEC_Q
    rw_params = (rw_mu[l], rw_w0[l], rw_w_up[l], rw_a0[l], rw_a_up[l], rw_k_k[l], rw_k_a[l], rw_r_k[l],
                 rw_gn_w[l], rw_gn_b[l])
    w_in_b = w_in[l].astype(BF16)
    bias = _bias_tables(rel_table)
    far = rel_table[REL_BUCKETS - 1]

    xp = x_prompt.reshape(bp * tp, D_MODEL)
    zrw, grw, gat, m, k, v, qb, kb, vb = _in_proj(xp, norm_g[l], w_in_b, at_q_norm[l], at_k_norm[l], _row_tile(bp * tp))
    zrw3 = zrw.reshape(bp, tp, RW_SHIFT_WIDTH)
    y_rw, wkv_p = _rwkv(zrw3, jnp.zeros((bp, RW_SHIFT_WIDTH), F32),
                        jnp.zeros((bp, RW_HEADS, HEAD_DIM, HEAD_DIM), F32), rw_params, 64, 64)
    y_at = _moba_prompt(qb.reshape(bp, tp, AT_WIDTH), kb.reshape(bp, tp, AT_WIDTH), vb.reshape(bp, tp, AT_WIDTH),
                        bias, far)
    y_p = _out_proj(xp, y_rw.reshape(bp * tp, RW_WIDTH), grw, y_at.reshape(bp * tp, AT_WIDTH), gat, m,
                    w_o_rwkv[l], w_o_attn[l], w_out[l], _row_tile(bp * tp)).reshape(bp, tp, D_MODEL)
    k_p = k.reshape(1, bp, tp, AT_HEADS, HEAD_DIM)
    v_p = v.reshape(1, bp, tp, AT_HEADS, HEAD_DIM)
    shift_p = zrw3[:, tp - 1][None]

    xs = x_sample.reshape(bs * ts, D_MODEL)
    zrw, grw, gat, m, k, v, qb, kb, vb = _in_proj(xs, norm_g[l], w_in_b, at_q_norm[l], at_k_norm[l], _row_tile(bs * ts))
    zrw3 = zrw.reshape(bs, ts, RW_SHIFT_WIDTH)
    zpad = jnp.pad(zrw3, ((0, 0), (0, DEC_PAD - ts), (0, 0)))
    y_rw, wkv_s = _rwkv(zpad, state_shift[l], state_wkv[l], rw_params, DEC_PAD, ts)
    y_rw = y_rw[:, :ts]
    pad_rows = lambda a: jnp.pad(a.reshape(bs, ts, AT_WIDTH), ((0, 0), (0, DEC_PAD - ts), (0, 0)))
    blast = jnp.transpose(bias[:, 1, :, :DEC_Q], (1, 2, 0)).reshape(MOBA_BLOCK, DEC_COLS)
    bown = jnp.transpose(bias[:, 0, :DEC_PAD, :DEC_Q], (1, 2, 0)).reshape(DEC_PAD, DEC_COLS)
    far32 = jnp.tile(far, DEC_Q).reshape(1, DEC_COLS)
    y_at = _moba_decode(page_table, qb.reshape(bs, ts, AT_WIDTH), pad_rows(k), pad_rows(v),
                        cache_k[l], cache_v[l], blast, bown, far32)
    y_s = _out_proj(xs, y_rw.reshape(bs * ts, RW_WIDTH), grw, y_at.reshape(bs * ts, AT_WIDTH), gat, m,
                    w_o_rwkv[l], w_o_attn[l], w_out[l], _row_tile(bs * ts)).reshape(bs, ts, D_MODEL)
    k_s = k.reshape(1, bs, ts, AT_HEADS, HEAD_DIM)
    v_s = v.reshape(1, bs, ts, AT_HEADS, HEAD_DIM)
    shift_s = zrw3[:, ts - 1][None]

    return (y_p, y_s, k_p, v_p, shift_p, wkv_p[None], k_s, v_s, shift_s, wkv_s[None])
```

```python
import functools
import math

import jax
import jax.numpy as jnp
from jax import lax
from jax.experimental import pallas as pl
from jax.experimental.pallas import tpu as pltpu

D_MODEL = 1024
HEAD_DIM = 64
RW_WIDTH = 512
RW_HEADS = 8
RW_RANK = 64
RW_SHIFT_WIDTH = 3 * RW_WIDTH + 2 * RW_RANK
RW_GN_EPS = 64e-5
AT_WIDTH = 512
AT_HEADS = 8
MOBA_BLOCK = 256
MOBA_TOPK = 3
PAGE_SIZE = 128
REL_BUCKETS = 32
REL_MAX_DIST = 128
NORM_EPS = 1e-6
NEG_BIG = -1e30
IN_WIDTH = RW_SHIFT_WIDTH + RW_WIDTH + 4 * AT_WIDTH + 2 * D_MODEL
C_GRW = RW_SHIFT_WIDTH
C_Q = C_GRW + RW_WIDTH
C_K = C_Q + AT_WIDTH
C_V = C_K + AT_WIDTH
C_GAT = C_V + AT_WIDTH
C_M = C_GAT + AT_WIDTH

VMEM_LIMIT_V7X = 56 * 1024 * 1024
LANES = 128
LOG2E = math.log2(math.e)
QK_SCALE = HEAD_DIM ** -0.5 * LOG2E
F32 = jnp.float32
BF16 = jnp.bfloat16
HIGHEST = lax.Precision.HIGHEST

NN = (((1,), (0,)), ((), ()))
NT = (((1,), (1,)), ((), ()))
TN = (((0,), (0,)), ((), ()))


def _dot(a, b, dims=NN, precision=None):
    return lax.dot_general(a, b, dims, precision=precision, preferred_element_type=F32)


def _split_dot(x, ones_bf16, passes):
    acc = None
    rem = x
    for _ in range(passes):
        part = rem.astype(BF16)
        rem = rem - part.astype(F32)
        term = _dot(part, ones_bf16)
        acc = term if acc is None else acc + term
    return acc


def _sigmoid(x):
    return 1.0 / (1.0 + jnp.exp(-x))


def _silu(x):
    return x * _sigmoid(x)


def _head_ones():
    r = lax.broadcasted_iota(jnp.int32, (RW_WIDTH, RW_WIDTH), 0) // HEAD_DIM
    c = lax.broadcasted_iota(jnp.int32, (RW_WIDTH, RW_WIDTH), 1) // HEAD_DIM
    return (r == c).astype(BF16)


def _in_proj_kernel(x_ref, g_ref, w_ref, qn_ref, kn_ref,
                    zrw_ref, grw_ref, gat_ref, m_ref, k_ref, v_ref, qb_ref, kb_ref, vb_ref):
    x = x_ref[...]
    ms = jnp.mean(x * x, axis=-1, keepdims=True)
    h = (x * lax.rsqrt(ms + NORM_EPS) * g_ref[...]).astype(BF16)
    ones = _head_ones()

    def proj(lo, hi):
        return _dot(h, w_ref[:, lo:hi])

    def head_norm(z, gain):
        msq = _split_dot(z * z, ones, 3) * (1.0 / HEAD_DIM)
        return z * lax.rsqrt(msq + NORM_EPS) * gain

    zrw_ref[...] = proj(0, C_GRW)
    grw_ref[...] = proj(C_GRW, C_Q)
    q = head_norm(proj(C_Q, C_K), qn_ref[...])
    qb_ref[...] = (q * QK_SCALE).astype(BF16)
    k = head_norm(proj(C_K, C_V), kn_ref[...])
    k_ref[...] = k
    kb_ref[...] = k.astype(BF16)
    v = proj(C_V, C_GAT)
    v_ref[...] = v
    vb_ref[...] = v.astype(BF16)
    gat_ref[...] = proj(C_GAT, C_M)
    m_ref[...] = proj(C_M, IN_WIDTH)


def _in_proj(x2d, norm_g, w_bf16, q_norm, k_norm, tm):
    n = x2d.shape[0]
    row = lambda w: pl.BlockSpec((tm, w), lambda i: (i, 0))
    full = lambda a: pl.BlockSpec(a.shape, lambda i: (0,) * a.ndim)
    g2 = norm_g.reshape(1, D_MODEL)
    qn = jnp.tile(q_norm, AT_HEADS).reshape(1, AT_WIDTH)
    kn = jnp.tile(k_norm, AT_HEADS).reshape(1, AT_WIDTH)
    widths = (RW_SHIFT_WIDTH, RW_WIDTH, AT_WIDTH, 2 * D_MODEL, AT_WIDTH, AT_WIDTH)
    out_shape = [jax.ShapeDtypeStruct((n, w), F32) for w in widths]
    out_shape += [jax.ShapeDtypeStruct((n, AT_WIDTH), BF16)] * 3
    out_specs = [row(w) for w in widths] + [row(AT_WIDTH)] * 3
    return pl.pallas_call(
        _in_proj_kernel,
        grid=(n // tm,),
        in_specs=[row(D_MODEL), full(g2), full(w_bf16), full(qn), full(kn)],
        out_specs=out_specs,
        out_shape=out_shape,
        compiler_params=pltpu.CompilerParams(dimension_semantics=("parallel",),
                                             vmem_limit_bytes=VMEM_LIMIT_V7X),
        name="in_proj",
    )(x2d, g2, w_bf16, qn, kn)


def _seg_sum(x, seg, seg_t):
    return _split_dot(_split_dot(x, seg, 3), seg_t, 3)


def _rwkv_kernel(z_ref, zlast_ref, s0_ref, mu_ref, w0_ref, wup_ref, a0_ref, aup_ref, kk_ref, ka_ref,
                 rk_ref, gnw_ref, gnb_ref, y_ref, sout_ref, s_scr, zprev_scr, *, chunk, n_valid):
    c = pl.program_id(1)

    @pl.when(c == 0)
    def _():
        s_scr[...] = s0_ref[0]
        zprev_scr[...] = zlast_ref[0]

    z = z_ref[0]
    row = lax.broadcasted_iota(jnp.int32, (chunk, 1), 0)
    zp = jnp.where(row == 0, zprev_scr[...], pltpu.roll(z, 1, axis=0))
    zprev_scr[...] = z[n_valid - 1:n_valid]
    zl = z + (zp - z) * mu_ref[...]
    r = zl[:, 0:RW_WIDTH]
    k = zl[:, RW_WIDTH:2 * RW_WIDTH]
    v = zl[:, 2 * RW_WIDTH:3 * RW_WIDTH]
    wd = zl[:, 3 * RW_WIDTH:3 * RW_WIDTH + RW_RANK]
    ad = zl[:, 3 * RW_WIDTH + RW_RANK:RW_SHIFT_WIDTH]

    x = -(w0_ref[...] + _dot(jnp.tanh(wd).astype(BF16), wup_ref[...]))
    softplus = jnp.maximum(x, 0.0) + jnp.log(1.0 + jnp.exp(-jnp.abs(x)))
    w = -softplus - 0.5
    ld = -jnp.exp(w)
    a = _sigmoid(a0_ref[...] + _dot(ad.astype(BF16), aup_ref[...]))
    seg = (lax.broadcasted_iota(jnp.int32, (RW_WIDTH, LANES), 0) // HEAD_DIM
           == lax.broadcasted_iota(jnp.int32, (RW_WIDTH, LANES), 1)).astype(BF16)
    seg_t = (lax.broadcasted_iota(jnp.int32, (LANES, RW_WIDTH), 1) // HEAD_DIM
             == lax.broadcasted_iota(jnp.int32, (LANES, RW_WIDTH), 0)).astype(BF16)
    kk = k * kk_ref[...]
    kk = kk / jnp.maximum(jnp.sqrt(_seg_sum(kk * kk, seg, seg_t)), 1e-12)
    kmod = k * (1.0 + (a - 1.0) * ka_ref[...])
    b = kk * a
    if n_valid < chunk:
        ok = row < n_valid
        ld = jnp.where(ok, ld, 0.0)
        kk = jnp.where(ok, kk, 0.0)
        kmod = jnp.where(ok, kmod, 0.0)
        b = jnp.where(ok, b, 0.0)

    ti = lax.broadcasted_iota(jnp.int32, (chunk, chunk), 0)
    tj = lax.broadcasted_iota(jnp.int32, (chunk, chunk), 1)
    incl = ti >= tj
    strict = ti > tj
    eye = (ti == tj).astype(F32)
    tri = incl.astype(BF16)
    cum = None
    rem = ld
    for _ in range(3):
        part = rem.astype(BF16)
        rem = rem - part.astype(F32)
        term = _dot(tri, part)
        cum = term if cum is None else cum + term
    tot = cum[chunk - 1:chunk]
    kkt = (kk * jnp.exp(cum - ld)).astype(BF16)
    rt = (r * jnp.exp(cum)).astype(BF16)
    pinv = jnp.exp(-cum)
    kh = (kmod * pinv).astype(BF16)
    bh = (b * pinv).astype(BF16)
    pend = jnp.exp(tot - cum)
    kp = (kmod * pend).astype(BF16)
    bp = (b * pend).astype(BF16)
    ptot = jnp.exp(tot)
    vb = v.astype(BF16)

    heads = range(RW_HEADS)
    hsl = [slice(hd * HEAD_DIM, (hd + 1) * HEAD_DIM) for hd in heads]
    s_old = [s_scr[hd] for hd in heads]
    v_h = [vb[:, sl] for sl in hsl]
    lhs = [jnp.concatenate([kkt[:, sl], rt[:, sl]], axis=0) for sl in hsl]
    rhs = [jnp.concatenate([kh[:, sl], bh[:, sl]], axis=0) for sl in hsl]
    g = [_dot(lhs[hd], rhs[hd], NT) for hd in heads]
    from_state = [_dot(lhs[hd], s_old[hd].astype(BF16), NT) for hd in heads]
    a_k = [jnp.where(strict, g[hd][:chunk, :chunk], 0.0).astype(BF16) for hd in heads]
    a_rk = [jnp.where(incl, g[hd][chunk:, :chunk], 0.0).astype(BF16) for hd in heads]
    a_rb = [jnp.where(incl, g[hd][chunk:, chunk:], 0.0).astype(BF16) for hd in heads]
    pw = [jnp.where(strict, -g[hd][:chunk, chunk:], 0.0) for hd in heads]
    tinv = [eye + pw[hd] for hd in heads]
    for _ in range(int(math.log2(chunk)) - 1):
        pwb = [pw[hd].astype(BF16) for hd in heads]
        pw = [_dot(pwb[hd], pwb[hd]) for hd in heads]
        tinv = [tinv[hd] + _dot(tinv[hd].astype(BF16), pw[hd].astype(BF16)) for hd in heads]
    rhs_u = [from_state[hd][:chunk] + _dot(a_k[hd], v_h[hd]) for hd in heads]
    y_part = [from_state[hd][chunk:] + _dot(a_rk[hd], v_h[hd]) for hd in heads]
    s_part = [s_old[hd] * ptot[:, hsl[hd]] + _dot(v_h[hd], kp[:, hsl[hd]], TN) for hd in heads]
    u = [_dot(tinv[hd].astype(BF16), rhs_u[hd].astype(BF16)).astype(BF16) for hd in heads]
    ys = [y_part[hd] - _dot(a_rb[hd], u[hd]) for hd in heads]
    for hd in heads:
        s_scr[hd] = s_part[hd] - _dot(u[hd], bp[:, hsl[hd]], TN)
    y = jnp.concatenate(ys, axis=1)

    inv_n = 1.0 / HEAD_DIM
    sums = _seg_sum(jnp.concatenate([y, r * kmod * rk_ref[...]], axis=0), seg, seg_t)
    yc = y - sums[:chunk] * inv_n
    var = _seg_sum(yc * yc, seg, seg_t) * inv_n
    y = yc * lax.rsqrt(var + RW_GN_EPS) * gnw_ref[...] + gnb_ref[...]
    y_ref[0] = y + sums[chunk:] * v

    @pl.when(c == pl.num_programs(1) - 1)
    def _():
        sout_ref[0] = s_scr[...]


def _rwkv(z3, zlast, s0, params, chunk, n_valid):
    bsz, t, _ = z3.shape
    nc = t // chunk
    full = lambda a: pl.BlockSpec(a.shape, lambda bi, ci: (0,) * a.ndim)
    mu, w0, wup, a0, aup, k_k, k_a, r_k, gn_w, gn_b = params
    ops = [mu.reshape(1, -1), w0.reshape(1, -1), wup.astype(BF16), a0.reshape(1, -1), aup.astype(BF16),
           k_k.reshape(1, -1), k_a.reshape(1, -1), r_k.reshape(1, -1), gn_w.reshape(1, -1), gn_b.reshape(1, -1)]
    kern = functools.partial(_rwkv_kernel, chunk=chunk, n_valid=n_valid)
    return pl.pallas_call(
        kern,
        grid=(bsz, nc),
        in_specs=[pl.BlockSpec((1, chunk, RW_SHIFT_WIDTH), lambda bi, ci: (bi, ci, 0)),
                  pl.BlockSpec((1, 1, RW_SHIFT_WIDTH), lambda bi, ci: (bi, 0, 0)),
                  pl.BlockSpec((1, RW_HEADS, HEAD_DIM, HEAD_DIM), lambda bi, ci: (bi, 0, 0, 0))]
                 + [full(o) for o in ops],
        out_specs=[pl.BlockSpec((1, chunk, RW_WIDTH), lambda bi, ci: (bi, ci, 0)),
                   pl.BlockSpec((1, RW_HEADS, HEAD_DIM, HEAD_DIM), lambda bi, ci: (bi, 0, 0, 0))],
        out_shape=[jax.ShapeDtypeStruct((bsz, t, RW_WIDTH), F32),
                   jax.ShapeDtypeStruct((bsz, RW_HEADS, HEAD_DIM, HEAD_DIM), F32)],
        scratch_shapes=[pltpu.VMEM((RW_HEADS, HEAD_DIM, HEAD_DIM), F32),
                        pltpu.VMEM((1, RW_SHIFT_WIDTH), F32)],
        compiler_params=pltpu.CompilerParams(dimension_semantics=("parallel", "arbitrary"),
                                             vmem_limit_bytes=VMEM_LIMIT_V7X),
        name="rwkv",
    )(z3, zlast.reshape(bsz, 1, RW_SHIFT_WIDTH), s0, *ops)


def _bias_kernel(rel_ref, o_ref):
    hd = pl.program_id(0)
    kj = lax.broadcasted_iota(jnp.int32, (MOBA_BLOCK, MOBA_BLOCK), 0)
    qi = lax.broadcasted_iota(jnp.int32, (MOBA_BLOCK, MOBA_BLOCK), 1)
    exact = REL_BUCKETS // 2
    for which in range(2):
        d = jnp.maximum(qi - kj + which * MOBA_BLOCK, 0)
        logd = jnp.log(jnp.maximum(d, 1).astype(F32) / exact) / math.log(REL_MAX_DIST / exact)
        large = jnp.minimum(exact + (logd * (REL_BUCKETS - exact)).astype(jnp.int32), REL_BUCKETS - 1)
        bucket = jnp.where(d < exact, d, large)
        acc = jnp.zeros((MOBA_BLOCK, MOBA_BLOCK), F32)
        for bkt in range(REL_BUCKETS):
            acc = jnp.where(bucket == bkt, rel_ref[bkt, hd], acc)
        acc = acc * LOG2E
        if which == 0:
            acc = jnp.where(kj <= qi, acc, NEG_BIG)
        o_ref[0, which] = acc
    o_ref[0, 2] = jnp.full((MOBA_BLOCK, MOBA_BLOCK), rel_ref[REL_BUCKETS - 1, hd], F32) * LOG2E


N_BIAS = 3


def _bias_tables(rel_table):
    return pl.pallas_call(
        _bias_kernel,
        grid=(AT_HEADS,),
        in_specs=[pl.BlockSpec(memory_space=pltpu.SMEM)],
        out_specs=pl.BlockSpec((1, N_BIAS, MOBA_BLOCK, MOBA_BLOCK), lambda h: (h, 0, 0, 0)),
        out_shape=jax.ShapeDtypeStruct((AT_HEADS, N_BIAS, MOBA_BLOCK, MOBA_BLOCK), F32),
        name="rel_bias",
    )(rel_table)


def _topk_mask(gate, valid, blk, nblk):
    rank = jnp.zeros(gate.shape, jnp.int32)
    for jp in range(nblk):
        gj = gate[jp:jp + 1, :]
        beats = (gj > gate) | ((gj == gate) & (jp < blk))
        rank = rank + beats.astype(jnp.int32)
    return valid & (rank < MOBA_TOPK)


MOBA_HEADS_PER_STEP = 4
ONES_ROWS = 16


def _moba_kernel(q_ref, k_ref, v_ref, bias_ref, o_ref, kmean_scr, vt_scr, mask_scr, *, nblk):
    i = pl.program_id(2)
    heads = range(MOBA_HEADS_PER_STEP)
    pair = 2 * MOBA_BLOCK

    @pl.when(i == 0)
    def _():
        kf = k_ref[0].astype(F32).reshape(nblk, MOBA_BLOCK, MOBA_HEADS_PER_STEP * HEAD_DIM)
        kmean_scr[...] = jnp.sum(kf, axis=1) * (1.0 / MOBA_BLOCK)
        for c0 in range(0, nblk * MOBA_BLOCK, pair):
            vt_scr[:, c0:c0 + pair] = jnp.transpose(v_ref[0, c0:c0 + pair, :].astype(F32)).astype(BF16)

    blk = lax.broadcasted_iota(jnp.int32, (nblk, MOBA_BLOCK), 0)
    q2 = q_ref[0].astype(F32)
    hsl = [slice(hh * HEAD_DIM, (hh + 1) * HEAD_DIM) for hh in heads]
    valid = blk < i
    qst = []
    for hh in heads:
        q = q2[:, hsl[hh]]
        gate = _dot(kmean_scr[:, hsl[hh]], q, NT, HIGHEST)
        gate = jnp.where(valid, gate, -jnp.inf)
        keep = _topk_mask(gate, valid, blk, nblk) | (blk == i)
        mask_scr[hh] = jnp.where(keep, 0.0, NEG_BIG)
        qst.append(jnp.transpose(q).astype(BF16))
    ones_rows = jnp.ones((ONES_ROWS, pair), BF16)

    def body(step, carry):
        pp = i // 2 - step
        r0 = pl.multiple_of(pp * pair, pair)
        j_lo = 2 * pp
        out = []
        s_parts = []
        for hh in heads:
            s = _dot(k_ref[0, pl.ds(r0, pair), hsl[hh]], qst[hh])
            parts = []
            for half in range(2):
                j = j_lo + half
                which = jnp.clip(i - j, 0, N_BIAS - 1)
                parts.append(s[half * MOBA_BLOCK:(half + 1) * MOBA_BLOCK] + bias_ref[hh, which]
                             + mask_scr[hh, pl.ds(j, 1), :])
            s_parts.append(parts)
        m_new = [jnp.maximum(carry[hh][0],
                             jnp.max(jnp.maximum(s_parts[hh][0], s_parts[hh][1]), axis=0, keepdims=True))
                 for hh in heads]
        pr = [jnp.concatenate([jnp.exp2(part - m_new[hh]) for part in s_parts[hh]], axis=0).astype(BF16)
              for hh in heads]
        pv = [_dot(jnp.concatenate([vt_scr[hsl[hh], pl.ds(r0, pair)], ones_rows], axis=0), pr[hh])
              for hh in heads]
        for hh in heads:
            m, acc = carry[hh]
            out.append((m_new[hh], jnp.exp2(m - m_new[hh]) * acc + pv[hh]))
        return tuple(out)

    init = tuple((jnp.full((1, MOBA_BLOCK), NEG_BIG, F32),
                  jnp.zeros((HEAD_DIM + ONES_ROWS, MOBA_BLOCK), F32)) for _ in heads)
    res = lax.fori_loop(0, i // 2 + 1, body, init)
    o_ref[0] = jnp.concatenate(
        [jnp.transpose(acc[:HEAD_DIM] / acc[HEAD_DIM:HEAD_DIM + 1]) for (_, acc) in res], axis=1)


def _moba_prompt(qb, kb, vb, bias):
    bsz, t, _ = qb.shape
    nblk = t // MOBA_BLOCK
    assert nblk % 2 == 0
    hw = MOBA_HEADS_PER_STEP * HEAD_DIM
    kern = functools.partial(_moba_kernel, nblk=nblk)
    return pl.pallas_call(
        kern,
        grid=(bsz, AT_HEADS // MOBA_HEADS_PER_STEP, nblk),
        in_specs=[pl.BlockSpec((1, MOBA_BLOCK, hw), lambda b, h, i: (b, i, h)),
                  pl.BlockSpec((1, t, hw), lambda b, h, i: (b, 0, h)),
                  pl.BlockSpec((1, t, hw), lambda b, h, i: (b, 0, h)),
                  pl.BlockSpec((MOBA_HEADS_PER_STEP, N_BIAS, MOBA_BLOCK, MOBA_BLOCK), lambda b, h, i: (h, 0, 0, 0))],
        out_specs=pl.BlockSpec((1, MOBA_BLOCK, hw), lambda b, h, i: (b, i, h)),
        out_shape=jax.ShapeDtypeStruct((bsz, t, AT_WIDTH), F32),
        scratch_shapes=[pltpu.VMEM((nblk, hw), F32),
                        pltpu.VMEM((hw, t), BF16),
                        pltpu.VMEM((MOBA_HEADS_PER_STEP, nblk, MOBA_BLOCK), F32)],
        compiler_params=pltpu.CompilerParams(dimension_semantics=("parallel", "parallel", "arbitrary"),
                                             vmem_limit_bytes=VMEM_LIMIT_V7X),
        name="moba_prompt",
    )(qb, kb, vb, bias)


DEC_Q = 4
DEC_ROWS = DEC_Q * AT_HEADS
DEC_PAD = 8
PAGES_PER_BLOCK = MOBA_BLOCK // PAGE_SIZE
BLOCK_LANES = MOBA_BLOCK * AT_HEADS


def _moba_decode_kernel(pt_ref, q_ref, knew_ref, vnew_ref, k0_ref, k1_ref, v0_ref, v1_ref,
                        bias_ref, bown_ref, o_ref, gate_scr, m_scr, l_scr, acc_scr, *, nblk):
    del pt_ref
    j = pl.program_id(1)
    col = lax.broadcasted_iota(jnp.int32, (DEC_ROWS, LANES), 1)

    @pl.when(j == 0)
    def _():
        gate_scr[...] = jnp.full((DEC_ROWS, LANES), -jnp.inf, F32)
        m_scr[...] = jnp.full((DEC_ROWS, LANES), NEG_BIG, F32)
        l_scr[...] = jnp.zeros((DEC_ROWS, LANES), F32)

    q = q_ref[0]

    def partial_softmax(kmat, vmat, bias):
        s = _dot(q, kmat.astype(BF16), NT) + bias
        m = jnp.max(s, axis=1, keepdims=True)
        pr = jnp.exp2(s - m)
        l = jnp.sum(pr, axis=1, keepdims=True)
        acc = _dot(pr.astype(BF16), vmat.astype(BF16))
        return m, l, acc

    def put(ref, column, value):
        ref[...] = jnp.where(col == column, value, ref[...])

    rows = PAGE_SIZE * AT_HEADS
    kmat = jnp.concatenate([k0_ref[0].reshape(rows, HEAD_DIM), k1_ref[0].reshape(rows, HEAD_DIM)], axis=0)
    vmat = jnp.concatenate([v0_ref[0].reshape(rows, HEAD_DIM), v1_ref[0].reshape(rows, HEAD_DIM)], axis=0)
    kmean = jnp.sum(kmat.reshape(MOBA_BLOCK, AT_HEADS, HEAD_DIM), axis=0) * (1.0 / MOBA_BLOCK)
    gate = jnp.sum(q.astype(F32) * jnp.concatenate([kmean] * DEC_Q, axis=0), axis=1, keepdims=True)
    which = (j == nblk - 1).astype(jnp.int32)
    m, l, acc = partial_softmax(kmat, vmat, bias_ref[which])
    put(gate_scr, j, gate)
    put(m_scr, j, m)
    put(l_scr, j, l)
    acc_scr[j] = acc

    @pl.when(j == nblk - 1)
    def _():
        m_o, l_o, acc_o = partial_softmax(knew_ref[0], vnew_ref[0], bown_ref[...])
        put(m_scr, nblk, m_o)
        put(l_scr, nblk, l_o)
        acc_scr[nblk] = acc_o
        gates = gate_scr[...]
        valid = col < nblk
        rank = jnp.zeros((DEC_ROWS, LANES), jnp.int32)
        for jp in range(nblk):
            gj = gates[:, jp:jp + 1]
            rank = rank + ((gj > gates) | ((gj == gates) & (jp < col))).astype(jnp.int32)
        sel = (valid & (rank < MOBA_TOPK)) | (col == nblk)
        m_all = m_scr[...]
        m_top = jnp.max(jnp.where(sel, m_all, NEG_BIG), axis=1, keepdims=True)
        wgt = jnp.where(sel, jnp.exp2(m_all - m_top), 0.0)
        denom = jnp.sum(wgt * l_scr[...], axis=1, keepdims=True)
        total = jnp.zeros((DEC_ROWS, HEAD_DIM), F32)
        for jj in range(nblk + 1):
            total = total + wgt[:, jj:jj + 1] * acc_scr[jj]
        o_ref[0] = total / denom


def _moba_decode(page_table, q32, k_new32, v_new32, cache_k, cache_v, bias2, bown):
    bsz = q32.shape[0]
    n_pages = page_table.shape[1]
    nblk = n_pages // PAGES_PER_BLOCK
    assert PAGES_PER_BLOCK == 2 and nblk + 1 <= LANES
    page = lambda off: pl.BlockSpec((1, PAGE_SIZE, AT_HEADS, HEAD_DIM),
                                    lambda b, j, pt: (pt[b, PAGES_PER_BLOCK * j + off], 0, 0, 0))
    per_b = pl.BlockSpec((1, DEC_ROWS, HEAD_DIM), lambda b, j, pt: (b, 0, 0))
    full = lambda a: pl.BlockSpec(a.shape, lambda b, j, pt: (0,) * a.ndim)
    kern = functools.partial(_moba_decode_kernel, nblk=nblk)
    grid_spec = pltpu.PrefetchScalarGridSpec(
        num_scalar_prefetch=1,
        grid=(bsz, nblk),
        in_specs=[per_b, per_b, per_b, page(0), page(1), page(0), page(1), full(bias2), full(bown)],
        out_specs=per_b,
        scratch_shapes=[pltpu.VMEM((DEC_ROWS, LANES), F32),
                        pltpu.VMEM((DEC_ROWS, LANES), F32),
                        pltpu.VMEM((DEC_ROWS, LANES), F32),
                        pltpu.VMEM((nblk + 1, DEC_ROWS, HEAD_DIM), F32)],
    )
    return pl.pallas_call(
        kern,
        grid_spec=grid_spec,
        out_shape=jax.ShapeDtypeStruct((bsz, DEC_ROWS, HEAD_DIM), F32),
        compiler_params=pltpu.CompilerParams(dimension_semantics=("parallel", "arbitrary"),
                                             vmem_limit_bytes=VMEM_LIMIT_V7X),
        name="moba_decode",
    )(page_table, q32, k_new32, v_new32, cache_k, cache_k, cache_v, cache_v, bias2, bown)


def _decode_bias(bias):
    neg = jnp.full((), NEG_BIG, F32)
    same = jnp.eye(AT_HEADS, dtype=bool)[None, :, None, :]
    last = jnp.transpose(bias[:, 1, :, :DEC_Q], (2, 0, 1))
    last = jnp.where(same, last[..., None], neg).reshape(DEC_ROWS, BLOCK_LANES)
    far = jnp.broadcast_to(bias[:, 2, 0, 0][None, :, None, None], (DEC_Q, AT_HEADS, MOBA_BLOCK, AT_HEADS))
    far = jnp.where(same, far, neg).reshape(DEC_ROWS, BLOCK_LANES)
    own = jnp.transpose(bias[:, 0, :DEC_Q, :DEC_Q], (2, 0, 1))
    own = jnp.where(same, own[..., None], neg).reshape(DEC_ROWS, DEC_ROWS)
    return jnp.stack([far, last]), own


def _out_proj_kernel(x_ref, yrw_ref, grw_ref, yat_ref, gat_ref, m_ref, worw_ref, woat_ref, wout_ref, o_ref):
    o_rw = _dot((yrw_ref[...] * _silu(grw_ref[...])).astype(BF16), worw_ref[...])
    o_at = _dot((yat_ref[...] * _silu(gat_ref[...])).astype(BF16), woat_ref[...])
    m = m_ref[...]
    merged = _sigmoid(m[:, :D_MODEL]) * o_rw + _sigmoid(m[:, D_MODEL:]) * o_at
    o_ref[...] = x_ref[...] + _dot(merged.astype(BF16), wout_ref[...])


def _out_proj(x2d, y_rw, g_rw, y_at, g_at, m, w_o_rwkv, w_o_attn, w_out, tm):
    n = x2d.shape[0]
    row = lambda w: pl.BlockSpec((tm, w), lambda i: (i, 0))
    full = lambda a: pl.BlockSpec(a.shape, lambda i: (0,) * a.ndim)
    ws = [w_o_rwkv.astype(BF16), w_o_attn.astype(BF16), w_out.astype(BF16)]
    return pl.pallas_call(
        _out_proj_kernel,
        grid=(n // tm,),
        in_specs=[row(D_MODEL), row(RW_WIDTH), row(RW_WIDTH), row(AT_WIDTH), row(AT_WIDTH), row(2 * D_MODEL)]
                 + [full(w) for w in ws],
        out_specs=row(D_MODEL),
        out_shape=jax.ShapeDtypeStruct((n, D_MODEL), F32),
        compiler_params=pltpu.CompilerParams(dimension_semantics=("parallel",),
                                             vmem_limit_bytes=VMEM_LIMIT_V7X),
        name="out_proj",
    )(x2d, y_rw, g_rw, y_at, g_at, m, *ws)


def _row_tile(n):
    return 256 if n % 256 == 0 else n


def kernel(x_prompt, x_sample, cache_k, cache_v, page_table, state_shift, state_wkv, rel_table, norm_g, w_in, rw_mu, rw_w0, rw_w_up, rw_a0, rw_a_up, rw_k_k, rw_k_a, rw_r_k, rw_gn_w, rw_gn_b, at_q_norm, at_k_norm, w_o_rwkv, w_o_attn, w_out):
    depth = w_in.shape[0]
    assert depth == 1
    l = 0
    bp, tp, _ = x_prompt.shape
    bs, ts, _ = x_sample.shape
    assert ts == DEC_Q
    rw_params = (rw_mu[l], rw_w0[l], rw_w_up[l], rw_a0[l], rw_a_up[l], rw_k_k[l], rw_k_a[l], rw_r_k[l],
                 rw_gn_w[l], rw_gn_b[l])
    w_in_b = w_in[l].astype(BF16)
    bias = _bias_tables(rel_table)

    xp = x_prompt.reshape(bp * tp, D_MODEL)
    zrw, grw, gat, m, k, v, qb, kb, vb = _in_proj(xp, norm_g[l], w_in_b, at_q_norm[l], at_k_norm[l], _row_tile(bp * tp))
    zrw3 = zrw.reshape(bp, tp, RW_SHIFT_WIDTH)
    y_rw, wkv_p = _rwkv(zrw3, jnp.zeros((bp, RW_SHIFT_WIDTH), F32),
                        jnp.zeros((bp, RW_HEADS, HEAD_DIM, HEAD_DIM), F32), rw_params, 64, 64)
    y_at = _moba_prompt(qb.reshape(bp, tp, AT_WIDTH), kb.reshape(bp, tp, AT_WIDTH), vb.reshape(bp, tp, AT_WIDTH),
                        bias)
    y_p = _out_proj(xp, y_rw.reshape(bp * tp, RW_WIDTH), grw, y_at.reshape(bp * tp, AT_WIDTH), gat, m,
                    w_o_rwkv[l], w_o_attn[l], w_out[l], _row_tile(bp * tp)).reshape(bp, tp, D_MODEL)
    k_p = k.reshape(1, bp, tp, AT_HEADS, HEAD_DIM)
    v_p = v.reshape(1, bp, tp, AT_HEADS, HEAD_DIM)
    shift_p = zrw3[:, tp - 1][None]

    xs = x_sample.reshape(bs * ts, D_MODEL)
    zrw, grw, gat, m, k, v, qb, kb, vb = _in_proj(xs, norm_g[l], w_in_b, at_q_norm[l], at_k_norm[l], _row_tile(bs * ts))
    zrw3 = zrw.reshape(bs, ts, RW_SHIFT_WIDTH)
    zpad = jnp.pad(zrw3, ((0, 0), (0, DEC_PAD - ts), (0, 0)))
    y_rw, wkv_s = _rwkv(zpad, state_shift[l], state_wkv[l], rw_params, DEC_PAD, ts)
    y_rw = y_rw[:, :ts]
    as_rows = lambda a: a.reshape(bs, DEC_ROWS, HEAD_DIM)
    bias2, bown = _decode_bias(bias)
    y_at = _moba_decode(page_table, as_rows(qb), as_rows(k), as_rows(v), cache_k[l], cache_v[l], bias2, bown)
    y_s = _out_proj(xs, y_rw.reshape(bs * ts, RW_WIDTH), grw, y_at.reshape(bs * ts, AT_WIDTH), gat, m,
                    w_o_rwkv[l], w_o_attn[l], w_out[l], _row_tile(bs * ts)).reshape(bs, ts, D_MODEL)
    k_s = k.reshape(1, bs, ts, AT_HEADS, HEAD_DIM)
    v_s = v.reshape(1, bs, ts, AT_HEADS, HEAD_DIM)
    shift_s = zrw3[:, ts - 1][None]

    return (y_p, y_s, k_p, v_p, shift_p, wkv_p[None], k_s, v_s, shift_s, wkv_s[None])
```

```python
import functools
import math

import jax
import jax.numpy as jnp
from jax import lax
from jax.experimental import pallas as pl
from jax.experimental.pallas import tpu as pltpu

D_MODEL = 1024
HEAD_DIM = 64
RW_WIDTH = 512
RW_HEADS = 8
RW_RANK = 64
RW_SHIFT_WIDTH = 3 * RW_WIDTH + 2 * RW_RANK
RW_GN_EPS = 64e-5
AT_WIDTH = 512
AT_HEADS = 8
MOBA_BLOCK = 256
MOBA_TOPK = 3
PAGE_SIZE = 128
REL_BUCKETS = 32
REL_MAX_DIST = 128
NORM_EPS = 1e-6
NEG_BIG = -1e30
IN_WIDTH = RW_SHIFT_WIDTH + RW_WIDTH + 4 * AT_WIDTH + 2 * D_MODEL
C_GRW = RW_SHIFT_WIDTH
C_Q = C_GRW + RW_WIDTH
C_K = C_Q + AT_WIDTH
C_V = C_K + AT_WIDTH
C_GAT = C_V + AT_WIDTH
C_M = C_GAT + AT_WIDTH

VMEM_LIMIT_V7X = 56 * 1024 * 1024
LANES = 128
LOG2E = math.log2(math.e)
QK_SCALE = HEAD_DIM ** -0.5 * LOG2E
F32 = jnp.float32
BF16 = jnp.bfloat16
HIGHEST = lax.Precision.HIGHEST

NN = (((1,), (0,)), ((), ()))
NT = (((1,), (1,)), ((), ()))
TN = (((0,), (0,)), ((), ()))


def _dot(a, b, dims=NN, precision=None):
    return lax.dot_general(a, b, dims, precision=precision, preferred_element_type=F32)


def _split_dot(x, ones_bf16, passes):
    acc = None
    rem = x
    for _ in range(passes):
        part = rem.astype(BF16)
        rem = rem - part.astype(F32)
        term = _dot(part, ones_bf16)
        acc = term if acc is None else acc + term
    return acc


def _sigmoid(x):
    return 1.0 / (1.0 + jnp.exp(-x))


def _silu(x):
    return x * _sigmoid(x)


def _head_ones():
    r = lax.broadcasted_iota(jnp.int32, (RW_WIDTH, RW_WIDTH), 0) // HEAD_DIM
    c = lax.broadcasted_iota(jnp.int32, (RW_WIDTH, RW_WIDTH), 1) // HEAD_DIM
    return (r == c).astype(BF16)


def _in_proj_kernel(x_ref, g_ref, w_ref, qn_ref, kn_ref,
                    zrw_ref, grw_ref, gat_ref, m_ref, k_ref, v_ref, qb_ref, kb_ref, vb_ref):
    x = x_ref[...]
    ms = jnp.mean(x * x, axis=-1, keepdims=True)
    h = (x * lax.rsqrt(ms + NORM_EPS) * g_ref[...]).astype(BF16)
    ones = _head_ones()

    def proj(lo, hi):
        return _dot(h, w_ref[:, lo:hi])

    def head_norm(z, gain):
        msq = _split_dot(z * z, ones, 3) * (1.0 / HEAD_DIM)
        return z * lax.rsqrt(msq + NORM_EPS) * gain

    zrw_ref[...] = proj(0, C_GRW)
    grw_ref[...] = proj(C_GRW, C_Q)
    q = head_norm(proj(C_Q, C_K), qn_ref[...])
    qb_ref[...] = (q * QK_SCALE).astype(BF16)
    k = head_norm(proj(C_K, C_V), kn_ref[...])
    k_ref[...] = k
    kb_ref[...] = k.astype(BF16)
    v = proj(C_V, C_GAT)
    v_ref[...] = v
    vb_ref[...] = v.astype(BF16)
    gat_ref[...] = proj(C_GAT, C_M)
    m_ref[...] = proj(C_M, IN_WIDTH)


def _in_proj(x2d, norm_g, w_bf16, q_norm, k_norm, tm):
    n = x2d.shape[0]
    row = lambda w: pl.BlockSpec((tm, w), lambda i: (i, 0))
    full = lambda a: pl.BlockSpec(a.shape, lambda i: (0,) * a.ndim)
    g2 = norm_g.reshape(1, D_MODEL)
    qn = jnp.tile(q_norm, AT_HEADS).reshape(1, AT_WIDTH)
    kn = jnp.tile(k_norm, AT_HEADS).reshape(1, AT_WIDTH)
    widths = (RW_SHIFT_WIDTH, RW_WIDTH, AT_WIDTH, 2 * D_MODEL, AT_WIDTH, AT_WIDTH)
    out_shape = [jax.ShapeDtypeStruct((n, w), F32) for w in widths]
    out_shape += [jax.ShapeDtypeStruct((n, AT_WIDTH), BF16)] * 3
    out_specs = [row(w) for w in widths] + [row(AT_WIDTH)] * 3
    return pl.pallas_call(
        _in_proj_kernel,
        grid=(n // tm,),
        in_specs=[row(D_MODEL), full(g2), full(w_bf16), full(qn), full(kn)],
        out_specs=out_specs,
        out_shape=out_shape,
        compiler_params=pltpu.CompilerParams(dimension_semantics=("parallel",),
                                             vmem_limit_bytes=VMEM_LIMIT_V7X),
        name="in_proj",
    )(x2d, g2, w_bf16, qn, kn)


def _seg_sum(x, seg, seg_t):
    return _split_dot(_split_dot(x, seg, 3), seg_t, 3)


def _rwkv_kernel(z_ref, zlast_ref, s0_ref, mu_ref, w0_ref, wup_ref, a0_ref, aup_ref, kk_ref, ka_ref,
                 rk_ref, gnw_ref, gnb_ref, y_ref, sout_ref, s_scr, zprev_scr, *, chunk, n_valid, nb):
    c = pl.program_id(1)

    @pl.when(c == 0)
    def _():
        s_scr[...] = s0_ref[...]
        zprev_scr[...] = zlast_ref[...]

    seqs = range(nb)
    rows = nb * chunk
    row = lax.broadcasted_iota(jnp.int32, (chunk, 1), 0)
    z_seq = [z_ref[bi] for bi in seqs]
    zp = jnp.concatenate([jnp.where(row == 0, zprev_scr[bi], pltpu.roll(z_seq[bi], 1, axis=0)) for bi in seqs],
                         axis=0)
    for bi in seqs:
        zprev_scr[bi] = z_seq[bi][n_valid - 1:n_valid]
    z = jnp.concatenate(z_seq, axis=0)
    row = lax.broadcasted_iota(jnp.int32, (rows, 1), 0) % chunk
    zl = z + (zp - z) * mu_ref[...]
    r = zl[:, 0:RW_WIDTH]
    k = zl[:, RW_WIDTH:2 * RW_WIDTH]
    v = zl[:, 2 * RW_WIDTH:3 * RW_WIDTH]
    wd = zl[:, 3 * RW_WIDTH:3 * RW_WIDTH + RW_RANK]
    ad = zl[:, 3 * RW_WIDTH + RW_RANK:RW_SHIFT_WIDTH]

    x = -(w0_ref[...] + _dot(jnp.tanh(wd).astype(BF16), wup_ref[...]))
    softplus = jnp.maximum(x, 0.0) + jnp.log(1.0 + jnp.exp(-jnp.abs(x)))
    w = -softplus - 0.5
    ld = -jnp.exp(w)
    a = _sigmoid(a0_ref[...] + _dot(ad.astype(BF16), aup_ref[...]))
    seg = (lax.broadcasted_iota(jnp.int32, (RW_WIDTH, LANES), 0) // HEAD_DIM
           == lax.broadcasted_iota(jnp.int32, (RW_WIDTH, LANES), 1)).astype(BF16)
    seg_t = (lax.broadcasted_iota(jnp.int32, (LANES, RW_WIDTH), 1) // HEAD_DIM
             == lax.broadcasted_iota(jnp.int32, (LANES, RW_WIDTH), 0)).astype(BF16)
    kk = k * kk_ref[...]
    kk = kk / jnp.maximum(jnp.sqrt(_seg_sum(kk * kk, seg, seg_t)), 1e-12)
    kmod = k * (1.0 + (a - 1.0) * ka_ref[...])
    b = kk * a
    if n_valid < chunk:
        ok = row < n_valid
        ld = jnp.where(ok, ld, 0.0)
        kk = jnp.where(ok, kk, 0.0)
        kmod = jnp.where(ok, kmod, 0.0)
        b = jnp.where(ok, b, 0.0)

    ti = lax.broadcasted_iota(jnp.int32, (chunk, chunk), 0)
    tj = lax.broadcasted_iota(jnp.int32, (chunk, chunk), 1)
    incl = ti >= tj
    strict = ti > tj
    eye = (ti == tj).astype(F32)
    fi = lax.broadcasted_iota(jnp.int32, (rows, rows), 0)
    fj = lax.broadcasted_iota(jnp.int32, (rows, rows), 1)
    tri = ((fi >= fj) & (fi // chunk == fj // chunk)).astype(BF16)
    cum = None
    rem = ld
    for _ in range(3):
        part = rem.astype(BF16)
        rem = rem - part.astype(F32)
        term = _dot(tri, part)
        cum = term if cum is None else cum + term
    tot_seq = [cum[(bi + 1) * chunk - 1:(bi + 1) * chunk] for bi in seqs]
    tot = jnp.concatenate([jnp.broadcast_to(t, (chunk, RW_WIDTH)) for t in tot_seq], axis=0)
    kkt = (kk * jnp.exp(cum - ld)).astype(BF16)
    rt = (r * jnp.exp(cum)).astype(BF16)
    pinv = jnp.exp(-cum)
    kh = (kmod * pinv).astype(BF16)
    bh = (b * pinv).astype(BF16)
    pend = jnp.exp(tot - cum)
    kp = (kmod * pend).astype(BF16)
    bp = (b * pend).astype(BF16)
    ptot = [jnp.exp(t) for t in tot_seq]
    vb = v.astype(BF16)

    chains = [(bi, hd) for bi in seqs for hd in range(RW_HEADS)]
    heads = range(len(chains))
    rsl = [slice(bi * chunk, (bi + 1) * chunk) for bi, _ in chains]
    hsl = [slice(hd * HEAD_DIM, (hd + 1) * HEAD_DIM) for _, hd in chains]
    s_old = [s_scr[bi, hd] for bi, hd in chains]
    v_h = [vb[rsl[ch], hsl[ch]] for ch in heads]
    lhs = [jnp.concatenate([kkt[rsl[ch], hsl[ch]], rt[rsl[ch], hsl[ch]]], axis=0) for ch in heads]
    rhs = [jnp.concatenate([kh[rsl[ch], hsl[ch]], bh[rsl[ch], hsl[ch]]], axis=0) for ch in heads]
    g = [_dot(lhs[hd], rhs[hd], NT) for hd in heads]
    from_state = [_dot(lhs[hd], s_old[hd].astype(BF16), NT) for hd in heads]
    a_k = [jnp.where(strict, g[hd][:chunk, :chunk], 0.0).astype(BF16) for hd in heads]
    a_rk = [jnp.where(incl, g[hd][chunk:, :chunk], 0.0).astype(BF16) for hd in heads]
    a_rb = [jnp.where(incl, g[hd][chunk:, chunk:], 0.0).astype(BF16) for hd in heads]
    pw = [jnp.where(strict, -g[hd][:chunk, chunk:], 0.0) for hd in heads]
    tinv = [eye + pw[hd] for hd in heads]
    for _ in range(int(math.log2(chunk)) - 1):
        pwb = [pw[hd].astype(BF16) for hd in heads]
        pw = [_dot(pwb[hd], pwb[hd]) for hd in heads]
        tinv = [tinv[hd] + _dot(tinv[hd].astype(BF16), pw[hd].astype(BF16)) for hd in heads]
    rhs_u = [from_state[hd][:chunk] + _dot(a_k[hd], v_h[hd]) for hd in heads]
    y_part = [from_state[hd][chunk:] + _dot(a_rk[hd], v_h[hd]) for hd in heads]
    s_part = [s_old[hd] * ptot[chains[hd][0]][:, hsl[hd]] + _dot(v_h[hd], kp[rsl[hd], hsl[hd]], TN) for hd in heads]
    u = [_dot(tinv[hd].astype(BF16), rhs_u[hd].astype(BF16)).astype(BF16) for hd in heads]
    ys = [y_part[hd] - _dot(a_rb[hd], u[hd]) for hd in heads]
    for ch, (bi, hd) in enumerate(chains):
        s_scr[bi, hd] = s_part[ch] - _dot(u[ch], bp[rsl[ch], hsl[ch]], TN)
    y = jnp.concatenate([jnp.concatenate(ys[bi * RW_HEADS:(bi + 1) * RW_HEADS], axis=1) for bi in seqs],
                        axis=0)

    inv_n = 1.0 / HEAD_DIM
    sums = _seg_sum(jnp.concatenate([y, r * kmod * rk_ref[...]], axis=0), seg, seg_t)
    yc = y - sums[:rows] * inv_n
    var = _seg_sum(yc * yc, seg, seg_t) * inv_n
    y = yc * lax.rsqrt(var + RW_GN_EPS) * gnw_ref[...] + gnb_ref[...]
    y = y + sums[rows:] * v
    for bi in seqs:
        y_ref[bi] = y[bi * chunk:(bi + 1) * chunk]

    @pl.when(c == pl.num_programs(1) - 1)
    def _():
        sout_ref[...] = s_scr[...]


RW_SEQS_PER_STEP = 4


def _rwkv(z3, zlast, s0, params, chunk, n_valid):
    bsz, t, _ = z3.shape
    nc = t // chunk
    nb = RW_SEQS_PER_STEP
    assert bsz % nb == 0
    full = lambda a: pl.BlockSpec(a.shape, lambda bi, ci: (0,) * a.ndim)
    mu, w0, wup, a0, aup, k_k, k_a, r_k, gn_w, gn_b = params
    ops = [mu.reshape(1, -1), w0.reshape(1, -1), wup.astype(BF16), a0.reshape(1, -1), aup.astype(BF16),
           k_k.reshape(1, -1), k_a.reshape(1, -1), r_k.reshape(1, -1), gn_w.reshape(1, -1), gn_b.reshape(1, -1)]
    kern = functools.partial(_rwkv_kernel, chunk=chunk, n_valid=n_valid, nb=nb)
    state_spec = pl.BlockSpec((nb, RW_HEADS, HEAD_DIM, HEAD_DIM), lambda bi, ci: (bi, 0, 0, 0))
    return pl.pallas_call(
        kern,
        grid=(bsz // nb, nc),
        in_specs=[pl.BlockSpec((nb, chunk, RW_SHIFT_WIDTH), lambda bi, ci: (bi, ci, 0)),
                  pl.BlockSpec((nb, 1, RW_SHIFT_WIDTH), lambda bi, ci: (bi, 0, 0)),
                  state_spec]
                 + [full(o) for o in ops],
        out_specs=[pl.BlockSpec((nb, chunk, RW_WIDTH), lambda bi, ci: (bi, ci, 0)), state_spec],
        out_shape=[jax.ShapeDtypeStruct((bsz, t, RW_WIDTH), F32),
                   jax.ShapeDtypeStruct((bsz, RW_HEADS, HEAD_DIM, HEAD_DIM), F32)],
        scratch_shapes=[pltpu.VMEM((nb, RW_HEADS, HEAD_DIM, HEAD_DIM), F32),
                        pltpu.VMEM((nb, 1, RW_SHIFT_WIDTH), F32)],
        compiler_params=pltpu.CompilerParams(dimension_semantics=("parallel", "arbitrary"),
                                             vmem_limit_bytes=VMEM_LIMIT_V7X),
        name="rwkv",
    )(z3, zlast.reshape(bsz, 1, RW_SHIFT_WIDTH), s0, *ops)


def _bias_kernel(rel_ref, o_ref):
    hd = pl.program_id(0)
    kj = lax.broadcasted_iota(jnp.int32, (MOBA_BLOCK, MOBA_BLOCK), 0)
    qi = lax.broadcasted_iota(jnp.int32, (MOBA_BLOCK, MOBA_BLOCK), 1)
    exact = REL_BUCKETS // 2
    for which in range(2):
        d = jnp.maximum(qi - kj + which * MOBA_BLOCK, 0)
        logd = jnp.log(jnp.maximum(d, 1).astype(F32) / exact) / math.log(REL_MAX_DIST / exact)
        large = jnp.minimum(exact + (logd * (REL_BUCKETS - exact)).astype(jnp.int32), REL_BUCKETS - 1)
        bucket = jnp.where(d < exact, d, large)
        acc = jnp.zeros((MOBA_BLOCK, MOBA_BLOCK), F32)
        for bkt in range(REL_BUCKETS):
            acc = jnp.where(bucket == bkt, rel_ref[bkt, hd], acc)
        acc = acc * LOG2E
        if which == 0:
            acc = jnp.where(kj <= qi, acc, NEG_BIG)
        o_ref[0, which] = acc
    o_ref[0, 2] = jnp.full((MOBA_BLOCK, MOBA_BLOCK), rel_ref[REL_BUCKETS - 1, hd], F32) * LOG2E


N_BIAS = 3


def _bias_tables(rel_table):
    return pl.pallas_call(
        _bias_kernel,
        grid=(AT_HEADS,),
        in_specs=[pl.BlockSpec(memory_space=pltpu.SMEM)],
        out_specs=pl.BlockSpec((1, N_BIAS, MOBA_BLOCK, MOBA_BLOCK), lambda h: (h, 0, 0, 0)),
        out_shape=jax.ShapeDtypeStruct((AT_HEADS, N_BIAS, MOBA_BLOCK, MOBA_BLOCK), F32),
        name="rel_bias",
    )(rel_table)


def _topk_mask(gate, valid, blk, nblk):
    rank = jnp.zeros(gate.shape, jnp.int32)
    for jp in range(nblk):
        gj = gate[jp:jp + 1, :]
        beats = (gj > gate) | ((gj == gate) & (jp < blk))
        rank = rank + beats.astype(jnp.int32)
    return valid & (rank < MOBA_TOPK)


MOBA_HEADS_PER_STEP = 4
ONES_ROWS = 16


def _moba_kernel(q_ref, k_ref, v_ref, bias_ref, o_ref, kmean_scr, vt_scr, mask_scr, *, nblk):
    i = pl.program_id(2)
    heads = range(MOBA_HEADS_PER_STEP)
    pair = 2 * MOBA_BLOCK

    @pl.when(i == 0)
    def _():
        kf = k_ref[0].astype(F32).reshape(nblk, MOBA_BLOCK, MOBA_HEADS_PER_STEP * HEAD_DIM)
        kmean_scr[...] = jnp.sum(kf, axis=1) * (1.0 / MOBA_BLOCK)
        for c0 in range(0, nblk * MOBA_BLOCK, pair):
            vt_scr[:, c0:c0 + pair] = jnp.transpose(v_ref[0, c0:c0 + pair, :].astype(F32)).astype(BF16)

    blk = lax.broadcasted_iota(jnp.int32, (nblk, MOBA_BLOCK), 0)
    q2 = q_ref[0].astype(F32)
    hsl = [slice(hh * HEAD_DIM, (hh + 1) * HEAD_DIM) for hh in heads]
    valid = blk < i
    qst = []
    for hh in heads:
        q = q2[:, hsl[hh]]
        gate = _dot(kmean_scr[:, hsl[hh]], q, NT, HIGHEST)
        gate = jnp.where(valid, gate, -jnp.inf)
        keep = _topk_mask(gate, valid, blk, nblk) | (blk == i)
        mask_scr[hh] = jnp.where(keep, 0.0, NEG_BIG)
        qst.append(jnp.transpose(q).astype(BF16))
    ones_rows = jnp.ones((ONES_ROWS, pair), BF16)

    def body(step, carry):
        pp = i // 2 - step
        r0 = pl.multiple_of(pp * pair, pair)
        j_lo = 2 * pp
        out = []
        s_parts = []
        for hh in heads:
            s = _dot(k_ref[0, pl.ds(r0, pair), hsl[hh]], qst[hh])
            parts = []
            for half in range(2):
                j = j_lo + half
                which = jnp.clip(i - j, 0, N_BIAS - 1)
                parts.append(s[half * MOBA_BLOCK:(half + 1) * MOBA_BLOCK] + bias_ref[hh, which]
                             + mask_scr[hh, pl.ds(j, 1), :])
            s_parts.append(parts)
        m_new = [jnp.maximum(carry[hh][0],
                             jnp.max(jnp.maximum(s_parts[hh][0], s_parts[hh][1]), axis=0, keepdims=True))
                 for hh in heads]
        pr = [jnp.concatenate([jnp.exp2(part - m_new[hh]) for part in s_parts[hh]], axis=0).astype(BF16)
              for hh in heads]
        pv = [_dot(jnp.concatenate([vt_scr[hsl[hh], pl.ds(r0, pair)], ones_rows], axis=0), pr[hh])
              for hh in heads]
        for hh in heads:
            m, acc = carry[hh]
            out.append((m_new[hh], jnp.exp2(m - m_new[hh]) * acc + pv[hh]))
        return tuple(out)

    init = tuple((jnp.full((1, MOBA_BLOCK), NEG_BIG, F32),
                  jnp.zeros((HEAD_DIM + ONES_ROWS, MOBA_BLOCK), F32)) for _ in heads)
    res = lax.fori_loop(0, i // 2 + 1, body, init)
    o_ref[0] = jnp.concatenate(
        [jnp.transpose(acc[:HEAD_DIM] / acc[HEAD_DIM:HEAD_DIM + 1]) for (_, acc) in res], axis=1)


def _moba_prompt(qb, kb, vb, bias):
    bsz, t, _ = qb.shape
    nblk = t // MOBA_BLOCK
    assert nblk % 2 == 0
    hw = MOBA_HEADS_PER_STEP * HEAD_DIM
    kern = functools.partial(_moba_kernel, nblk=nblk)
    return pl.pallas_call(
        kern,
        grid=(bsz, AT_HEADS // MOBA_HEADS_PER_STEP, nblk),
        in_specs=[pl.BlockSpec((1, MOBA_BLOCK, hw), lambda b, h, i: (b, i, h)),
                  pl.BlockSpec((1, t, hw), lambda b, h, i: (b, 0, h)),
                  pl.BlockSpec((1, t, hw), lambda b, h, i: (b, 0, h)),
                  pl.BlockSpec((MOBA_HEADS_PER_STEP, N_BIAS, MOBA_BLOCK, MOBA_BLOCK), lambda b, h, i: (h, 0, 0, 0))],
        out_specs=pl.BlockSpec((1, MOBA_BLOCK, hw), lambda b, h, i: (b, i, h)),
        out_shape=jax.ShapeDtypeStruct((bsz, t, AT_WIDTH), F32),
        scratch_shapes=[pltpu.VMEM((nblk, hw), F32),
                        pltpu.VMEM((hw, t), BF16),
                        pltpu.VMEM((MOBA_HEADS_PER_STEP, nblk, MOBA_BLOCK), F32)],
        compiler_params=pltpu.CompilerParams(dimension_semantics=("parallel", "parallel", "arbitrary"),
                                             vmem_limit_bytes=VMEM_LIMIT_V7X),
        name="moba_prompt",
    )(qb, kb, vb, bias)


DEC_Q = 4
DEC_ROWS = DEC_Q * AT_HEADS
DEC_PAD = 8
PAGES_PER_BLOCK = MOBA_BLOCK // PAGE_SIZE
DEC_BLOCKS_PER_STEP = 2
DEC_PAGES_PER_STEP = DEC_BLOCKS_PER_STEP * PAGES_PER_BLOCK


def _moba_decode_kernel(pt_ref, q_ref, knew_ref, vnew_ref, *rest, nblk):
    del pt_ref
    k_refs = rest[:DEC_PAGES_PER_STEP]
    v_refs = rest[DEC_PAGES_PER_STEP:2 * DEC_PAGES_PER_STEP]
    bias_ref, bown_ref, o_ref, qbd_scr, gate_scr, m_scr, l_scr, acc_scr = rest[2 * DEC_PAGES_PER_STEP:]
    step = pl.program_id(1)
    col = lax.broadcasted_iota(jnp.int32, (DEC_ROWS, LANES), 1)
    own_head = (lax.broadcasted_iota(jnp.int32, (DEC_ROWS, AT_WIDTH), 1) // HEAD_DIM
                == lax.broadcasted_iota(jnp.int32, (DEC_ROWS, AT_WIDTH), 0) % AT_HEADS)

    @pl.when(step == 0)
    def _():
        q = q_ref[0]
        rows = [jnp.broadcast_to(q[qq:qq + 1], (AT_HEADS, AT_WIDTH)) for qq in range(DEC_Q)]
        qbd_scr[...] = jnp.where(own_head, jnp.concatenate(rows, axis=0), jnp.zeros((), BF16))
        gate_scr[...] = jnp.full((DEC_ROWS, LANES), -jnp.inf, F32)
        m_scr[...] = jnp.full((DEC_ROWS, LANES), NEG_BIG, F32)
        l_scr[...] = jnp.zeros((DEC_ROWS, LANES), F32)

    qbd = qbd_scr[...]

    def put(ref, column, value):
        ref[...] = jnp.where(col == column, value, ref[...])

    def softmax_partial(j, s):
        m = jnp.max(s, axis=1, keepdims=True)
        pr = jnp.exp2(s - m)
        put(m_scr, j, m)
        put(l_scr, j, jnp.sum(pr, axis=1, keepdims=True))
        return pr.astype(BF16)

    blocks = range(DEC_BLOCKS_PER_STEP)
    js = [step * DEC_BLOCKS_PER_STEP + blk for blk in blocks]
    page_view = lambda ref: ref[0].reshape(AT_WIDTH, PAGE_SIZE).astype(BF16)
    raw = [jnp.concatenate([_dot(qbd, page_view(k_refs[blk * PAGES_PER_BLOCK + pg]))
                            for pg in range(PAGES_PER_BLOCK)], axis=1) for blk in blocks]
    weights = []
    for blk in blocks:
        put(gate_scr, js[blk], jnp.sum(raw[blk], axis=1, keepdims=True))
        which = (js[blk] == nblk - 1).astype(jnp.int32)
        weights.append(softmax_partial(js[blk], raw[blk] + bias_ref[which]))
    for blk in blocks:
        acc_scr[js[blk]] = sum(_dot(weights[blk][:, pg * PAGE_SIZE:(pg + 1) * PAGE_SIZE],
                                    page_view(v_refs[blk * PAGES_PER_BLOCK + pg]), NT)
                               for pg in range(PAGES_PER_BLOCK))

    @pl.when(step == pl.num_programs(1) - 1)
    def _():
        knew = knew_ref[0].astype(BF16)
        vnew = vnew_ref[0].astype(BF16)
        acc_scr[nblk] = _dot(softmax_partial(nblk, _dot(qbd, knew, NT) + bown_ref[...]), vnew)
        gates = gate_scr[...]
        valid = col < nblk
        rank = jnp.zeros((DEC_ROWS, LANES), jnp.int32)
        for jp in range(nblk):
            gj = gates[:, jp:jp + 1]
            rank = rank + ((gj > gates) | ((gj == gates) & (jp < col))).astype(jnp.int32)
        sel = (valid & (rank < MOBA_TOPK)) | (col == nblk)
        m_all = m_scr[...]
        m_top = jnp.max(jnp.where(sel, m_all, NEG_BIG), axis=1, keepdims=True)
        wgt = jnp.where(sel, jnp.exp2(m_all - m_top), 0.0)
        denom = jnp.sum(wgt * l_scr[...], axis=1, keepdims=True)
        total = jnp.zeros((DEC_ROWS, AT_WIDTH), F32)
        for jj in range(nblk + 1):
            total = total + wgt[:, jj:jj + 1] * acc_scr[jj]
        out = jnp.where(own_head, total / denom, 0.0)
        o_ref[0] = jnp.sum(out.reshape(DEC_Q, AT_HEADS, AT_WIDTH), axis=1)


def _moba_decode(page_table, qb, k_new, v_new, cache_k, cache_v, bias2, bown):
    bsz = qb.shape[0]
    n_pages = page_table.shape[1]
    nblk = n_pages // PAGES_PER_BLOCK
    assert nblk % DEC_BLOCKS_PER_STEP == 0 and nblk + 1 <= LANES
    kt = jnp.transpose(cache_k, (0, 2, 3, 1))
    vt = jnp.transpose(cache_v, (0, 2, 3, 1))
    page = lambda off: pl.BlockSpec((1, AT_HEADS, HEAD_DIM, PAGE_SIZE),
                                    lambda b, s, pt: (pt[b, DEC_PAGES_PER_STEP * s + off], 0, 0, 0))
    per_b = lambda rows: pl.BlockSpec((1, rows, AT_WIDTH), lambda b, s, pt: (b, 0, 0))
    full = lambda a: pl.BlockSpec(a.shape, lambda b, s, pt: (0,) * a.ndim)
    pages = [page(off) for off in range(DEC_PAGES_PER_STEP)]
    kern = functools.partial(_moba_decode_kernel, nblk=nblk)
    grid_spec = pltpu.PrefetchScalarGridSpec(
        num_scalar_prefetch=1,
        grid=(bsz, nblk // DEC_BLOCKS_PER_STEP),
        in_specs=[per_b(DEC_Q), per_b(DEC_PAD), per_b(DEC_PAD)] + pages + pages + [full(bias2), full(bown)],
        out_specs=per_b(DEC_Q),
        scratch_shapes=[pltpu.VMEM((DEC_ROWS, AT_WIDTH), BF16),
                        pltpu.VMEM((DEC_ROWS, LANES), F32),
                        pltpu.VMEM((DEC_ROWS, LANES), F32),
                        pltpu.VMEM((DEC_ROWS, LANES), F32),
                        pltpu.VMEM((nblk + 1, DEC_ROWS, AT_WIDTH), F32)],
    )
    return pl.pallas_call(
        kern,
        grid_spec=grid_spec,
        out_shape=jax.ShapeDtypeStruct((bsz, DEC_Q, AT_WIDTH), F32),
        compiler_params=pltpu.CompilerParams(dimension_semantics=("parallel", "arbitrary"),
                                             vmem_limit_bytes=VMEM_LIMIT_V7X),
        name="moba_decode",
    )(page_table, qb, k_new, v_new, *([kt] * DEC_PAGES_PER_STEP), *([vt] * DEC_PAGES_PER_STEP), bias2, bown)


def _decode_bias(bias):
    last = jnp.transpose(bias[:, 1, :, :DEC_Q], (2, 0, 1)).reshape(DEC_ROWS, MOBA_BLOCK)
    far = jnp.broadcast_to(bias[:, 2, 0, 0][None, :, None], (DEC_Q, AT_HEADS, MOBA_BLOCK)).reshape(DEC_ROWS, MOBA_BLOCK)
    own = jnp.transpose(bias[:, 0, :DEC_PAD, :DEC_Q], (2, 0, 1)).reshape(DEC_ROWS, DEC_PAD)
    return jnp.stack([far, last]), own


def _out_proj_kernel(x_ref, yrw_ref, grw_ref, yat_ref, gat_ref, m_ref, worw_ref, woat_ref, wout_ref, o_ref):
    o_rw = _dot((yrw_ref[...] * _silu(grw_ref[...])).astype(BF16), worw_ref[...])
    o_at = _dot((yat_ref[...] * _silu(gat_ref[...])).astype(BF16), woat_ref[...])
    m = m_ref[...]
    merged = _sigmoid(m[:, :D_MODEL]) * o_rw + _sigmoid(m[:, D_MODEL:]) * o_at
    o_ref[...] = x_ref[...] + _dot(merged.astype(BF16), wout_ref[...])


def _out_proj(x2d, y_rw, g_rw, y_at, g_at, m, w_o_rwkv, w_o_attn, w_out, tm):
    n = x2d.shape[0]
    row = lambda w: pl.BlockSpec((tm, w), lambda i: (i, 0))
    full = lambda a: pl.BlockSpec(a.shape, lambda i: (0,) * a.ndim)
    ws = [w_o_rwkv.astype(BF16), w_o_attn.astype(BF16), w_out.astype(BF16)]
    return pl.pallas_call(
        _out_proj_kernel,
        grid=(n // tm,),
        in_specs=[row(D_MODEL), row(RW_WIDTH), row(RW_WIDTH), row(AT_WIDTH), row(AT_WIDTH), row(2 * D_MODEL)]
                 + [full(w) for w in ws],
        out_specs=row(D_MODEL),
        out_shape=jax.ShapeDtypeStruct((n, D_MODEL), F32),
        compiler_params=pltpu.CompilerParams(dimension_semantics=("parallel",),
                                             vmem_limit_bytes=VMEM_LIMIT_V7X),
        name="out_proj",
    )(x2d, y_rw, g_rw, y_at, g_at, m, *ws)


def _row_tile(n):
    return 256 if n % 256 == 0 else n


def kernel(x_prompt, x_sample, cache_k, cache_v, page_table, state_shift, state_wkv, rel_table, norm_g, w_in, rw_mu, rw_w0, rw_w_up, rw_a0, rw_a_up, rw_k_k, rw_k_a, rw_r_k, rw_gn_w, rw_gn_b, at_q_norm, at_k_norm, w_o_rwkv, w_o_attn, w_out):
    depth = w_in.shape[0]
    assert depth == 1
    l = 0
    bp, tp, _ = x_prompt.shape
    bs, ts, _ = x_sample.shape
    assert ts == DEC_Q
    rw_params = (rw_mu[l], rw_w0[l], rw_w_up[l], rw_a0[l], rw_a_up[l], rw_k_k[l], rw_k_a[l], rw_r_k[l],
                 rw_gn_w[l], rw_gn_b[l])
    w_in_b = w_in[l].astype(BF16)
    bias = _bias_tables(rel_table)

    xp = x_prompt.reshape(bp * tp, D_MODEL)
    zrw, grw, gat, m, k, v, qb, kb, vb = _in_proj(xp, norm_g[l], w_in_b, at_q_norm[l], at_k_norm[l], _row_tile(bp * tp))
    zrw3 = zrw.reshape(bp, tp, RW_SHIFT_WIDTH)
    y_rw, wkv_p = _rwkv(zrw3, jnp.zeros((bp, RW_SHIFT_WIDTH), F32),
                        jnp.zeros((bp, RW_HEADS, HEAD_DIM, HEAD_DIM), F32), rw_params, 64, 64)
    y_at = _moba_prompt(qb.reshape(bp, tp, AT_WIDTH), kb.reshape(bp, tp, AT_WIDTH), vb.reshape(bp, tp, AT_WIDTH),
                        bias)
    y_p = _out_proj(xp, y_rw.reshape(bp * tp, RW_WIDTH), grw, y_at.reshape(bp * tp, AT_WIDTH), gat, m,
                    w_o_rwkv[l], w_o_attn[l], w_out[l], _row_tile(bp * tp)).reshape(bp, tp, D_MODEL)
    k_p = k.reshape(1, bp, tp, AT_HEADS, HEAD_DIM)
    v_p = v.reshape(1, bp, tp, AT_HEADS, HEAD_DIM)
    shift_p = zrw3[:, tp - 1][None]

    xs = x_sample.reshape(bs * ts, D_MODEL)
    zrw, grw, gat, m, k, v, qb, kb, vb = _in_proj(xs, norm_g[l], w_in_b, at_q_norm[l], at_k_norm[l], _row_tile(bs * ts))
    zrw3 = zrw.reshape(bs, ts, RW_SHIFT_WIDTH)
    zpad = jnp.pad(zrw3, ((0, 0), (0, DEC_PAD - ts), (0, 0)))
    y_rw, wkv_s = _rwkv(zpad, state_shift[l], state_wkv[l], rw_params, DEC_PAD, ts)
    y_rw = y_rw[:, :ts]
    pad_rows = lambda a: jnp.pad(a.reshape(bs, ts, AT_WIDTH), ((0, 0), (0, DEC_PAD - ts), (0, 0)))
    bias2, bown = _decode_bias(bias)
    y_at = _moba_decode(page_table, qb.reshape(bs, ts, AT_WIDTH), pad_rows(k), pad_rows(v),
                        cache_k[l], cache_v[l], bias2, bown)
    y_s = _out_proj(xs, y_rw.reshape(bs * ts, RW_WIDTH), grw, y_at.reshape(bs * ts, AT_WIDTH), gat, m,
                    w_o_rwkv[l], w_o_attn[l], w_out[l], _row_tile(bs * ts)).reshape(bs, ts, D_MODEL)
    k_s = k.reshape(1, bs, ts, AT_HEADS, HEAD_DIM)
    v_s = v.reshape(1, bs, ts, AT_HEADS, HEAD_DIM)
    shift_s = zrw3[:, ts - 1][None]

    return (y_p, y_s, k_p, v_p, shift_p, wkv_p[None], k_s, v_s, shift_s, wkv_s[None])
```

```python
import functools
import math

import jax
import jax.numpy as jnp
from jax import lax
from jax.experimental import pallas as pl
from jax.experimental.pallas import tpu as pltpu

D_MODEL = 1024
HEAD_DIM = 64
RW_WIDTH = 512
RW_HEADS = 8
RW_RANK = 64
RW_SHIFT_WIDTH = 3 * RW_WIDTH + 2 * RW_RANK
RW_GN_EPS = 64e-5
AT_WIDTH = 512
AT_HEADS = 8
MOBA_BLOCK = 256
MOBA_TOPK = 3
PAGE_SIZE = 128
REL_BUCKETS = 32
REL_MAX_DIST = 128
NORM_EPS = 1e-6
NEG_BIG = -1e30
IN_WIDTH = RW_SHIFT_WIDTH + RW_WIDTH + 4 * AT_WIDTH + 2 * D_MODEL
C_GRW = RW_SHIFT_WIDTH
C_Q = C_GRW + RW_WIDTH
C_K = C_Q + AT_WIDTH
C_V = C_K + AT_WIDTH
C_GAT = C_V + AT_WIDTH
C_M = C_GAT + AT_WIDTH

VMEM_LIMIT_V7X = 56 * 1024 * 1024
LANES = 128
LOG2E = math.log2(math.e)
QK_SCALE = HEAD_DIM ** -0.5 * LOG2E
F32 = jnp.float32
BF16 = jnp.bfloat16
HIGHEST = lax.Precision.HIGHEST

NN = (((1,), (0,)), ((), ()))
NT = (((1,), (1,)), ((), ()))
TN = (((0,), (0,)), ((), ()))


def _dot(a, b, dims=NN, precision=None):
    return lax.dot_general(a, b, dims, precision=precision, preferred_element_type=F32)


def _split_dot(x, ones_bf16, passes):
    acc = None
    rem = x
    for _ in range(passes):
        part = rem.astype(BF16)
        rem = rem - part.astype(F32)
        term = _dot(part, ones_bf16)
        acc = term if acc is None else acc + term
    return acc


def _sigmoid(x):
    return 1.0 / (1.0 + jnp.exp(-x))


def _silu(x):
    return x * _sigmoid(x)


def _head_ones():
    r = lax.broadcasted_iota(jnp.int32, (RW_WIDTH, RW_WIDTH), 0) // HEAD_DIM
    c = lax.broadcasted_iota(jnp.int32, (RW_WIDTH, RW_WIDTH), 1) // HEAD_DIM
    return (r == c).astype(BF16)


def _in_proj_kernel(x_ref, g_ref, w_ref, qn_ref, kn_ref,
                    zrw_ref, grw_ref, gat_ref, m_ref, k_ref, v_ref, qb_ref, kb_ref, vb_ref, *, token_minor):
    x = x_ref[...]
    ms = jnp.mean(x * x, axis=-1, keepdims=True)
    h = (x * lax.rsqrt(ms + NORM_EPS) * g_ref[...]).astype(BF16)
    ones = _head_ones()

    def proj(lo, hi):
        return _dot(h, w_ref[:, lo:hi])

    def head_norm(z, gain):
        msq = _split_dot(z * z, ones, 3) * (1.0 / HEAD_DIM)
        return z * lax.rsqrt(msq + NORM_EPS) * gain

    zrw_ref[...] = proj(0, C_GRW)
    grw_ref[...] = proj(C_GRW, C_Q).astype(BF16)
    q = head_norm(proj(C_Q, C_K), qn_ref[...])
    qb_ref[...] = (q * QK_SCALE).astype(BF16)
    k = head_norm(proj(C_K, C_V), kn_ref[...])
    kb_ref[...] = k.astype(BF16)
    v = proj(C_V, C_GAT)
    if token_minor:
        k_ref[0] = jnp.transpose(k)
        vt = jnp.transpose(v)
        v_ref[0] = vt
        vb_ref[0] = vt.astype(BF16)
    else:
        k_ref[...] = k
        v_ref[...] = v
        vb_ref[...] = v.astype(BF16)
    gat_ref[...] = proj(C_GAT, C_M).astype(BF16)
    m_ref[...] = proj(C_M, IN_WIDTH).astype(BF16)


def _in_proj(x2d, norm_g, w_bf16, q_norm, k_norm, tm, seq_len=None):
    n = x2d.shape[0]
    row = lambda w: pl.BlockSpec((tm, w), lambda i: (i, 0))
    full = lambda a: pl.BlockSpec(a.shape, lambda i: (0,) * a.ndim)
    g2 = norm_g.reshape(1, D_MODEL)
    qn = jnp.tile(q_norm, AT_HEADS).reshape(1, AT_WIDTH)
    kn = jnp.tile(k_norm, AT_HEADS).reshape(1, AT_WIDTH)
    widths = (RW_SHIFT_WIDTH, RW_WIDTH, AT_WIDTH, 2 * D_MODEL)
    if seq_len is None:
        kv_shape, kv_spec = (n, AT_WIDTH), row(AT_WIDTH)
    else:
        tiles = seq_len // tm
        assert seq_len % tm == 0
        kv_shape = (n // seq_len, AT_WIDTH, seq_len)
        kv_spec = pl.BlockSpec((1, AT_WIDTH, tm), lambda i: (i // tiles, 0, i % tiles))
    out_shape = ([jax.ShapeDtypeStruct((n, w), F32 if idx == 0 else BF16) for idx, w in enumerate(widths)]
                 + [jax.ShapeDtypeStruct(kv_shape, F32)] * 2
                 + [jax.ShapeDtypeStruct((n, AT_WIDTH), BF16)] * 2 + [jax.ShapeDtypeStruct(kv_shape, BF16)])
    out_specs = [row(w) for w in widths] + [kv_spec] * 2 + [row(AT_WIDTH)] * 2 + [kv_spec]
    return pl.pallas_call(
        functools.partial(_in_proj_kernel, token_minor=seq_len is not None),
        grid=(n // tm,),
        in_specs=[row(D_MODEL), full(g2), full(w_bf16), full(qn), full(kn)],
        out_specs=out_specs,
        out_shape=out_shape,
        compiler_params=pltpu.CompilerParams(dimension_semantics=("parallel",),
                                             vmem_limit_bytes=VMEM_LIMIT_V7X),
        name="in_proj",
    )(x2d, g2, w_bf16, qn, kn)


def _seg_sum(x, seg, seg_t):
    return _split_dot(_split_dot(x, seg, 3), seg_t, 3)


def _rwkv_kernel(z_ref, zlast_ref, s0_ref, mu_ref, w0_ref, wup_ref, a0_ref, aup_ref, kk_ref, ka_ref,
                 rk_ref, gnw_ref, gnb_ref, y_ref, sout_ref, s_scr, zprev_scr, *, chunk, n_valid, nb):
    c = pl.program_id(1)

    @pl.when(c == 0)
    def _():
        s_scr[...] = s0_ref[...]
        zprev_scr[...] = zlast_ref[...]

    seqs = range(nb)
    rows = nb * chunk
    row = lax.broadcasted_iota(jnp.int32, (chunk, 1), 0)
    z_seq = [z_ref[bi] for bi in seqs]
    zp = jnp.concatenate([jnp.where(row == 0, zprev_scr[bi], pltpu.roll(z_seq[bi], 1, axis=0)) for bi in seqs],
                         axis=0)
    for bi in seqs:
        zprev_scr[bi] = z_seq[bi][n_valid - 1:n_valid]
    z = jnp.concatenate(z_seq, axis=0)
    row = lax.broadcasted_iota(jnp.int32, (rows, 1), 0) % chunk
    zl = z + (zp - z) * mu_ref[...]
    r = zl[:, 0:RW_WIDTH]
    k = zl[:, RW_WIDTH:2 * RW_WIDTH]
    v = zl[:, 2 * RW_WIDTH:3 * RW_WIDTH]
    wd = zl[:, 3 * RW_WIDTH:3 * RW_WIDTH + RW_RANK]
    ad = zl[:, 3 * RW_WIDTH + RW_RANK:RW_SHIFT_WIDTH]

    x = -(w0_ref[...] + _dot(jnp.tanh(wd).astype(BF16), wup_ref[...]))
    softplus = jnp.maximum(x, 0.0) + jnp.log(1.0 + jnp.exp(-jnp.abs(x)))
    w = -softplus - 0.5
    ld = -jnp.exp(w)
    a = _sigmoid(a0_ref[...] + _dot(ad.astype(BF16), aup_ref[...]))
    seg = (lax.broadcasted_iota(jnp.int32, (RW_WIDTH, LANES), 0) // HEAD_DIM
           == lax.broadcasted_iota(jnp.int32, (RW_WIDTH, LANES), 1)).astype(BF16)
    seg_t = (lax.broadcasted_iota(jnp.int32, (LANES, RW_WIDTH), 1) // HEAD_DIM
             == lax.broadcasted_iota(jnp.int32, (LANES, RW_WIDTH), 0)).astype(BF16)
    kk = k * kk_ref[...]
    kk = kk / jnp.maximum(jnp.sqrt(_seg_sum(kk * kk, seg, seg_t)), 1e-12)
    kmod = k * (1.0 + (a - 1.0) * ka_ref[...])
    b = kk * a
    if n_valid < chunk:
        ok = row < n_valid
        ld = jnp.where(ok, ld, 0.0)
        kk = jnp.where(ok, kk, 0.0)
        kmod = jnp.where(ok, kmod, 0.0)
        b = jnp.where(ok, b, 0.0)

    ti = lax.broadcasted_iota(jnp.int32, (chunk, chunk), 0)
    tj = lax.broadcasted_iota(jnp.int32, (chunk, chunk), 1)
    incl = ti >= tj
    strict = ti > tj
    eye = (ti == tj).astype(F32)
    fi = lax.broadcasted_iota(jnp.int32, (rows, rows), 0)
    fj = lax.broadcasted_iota(jnp.int32, (rows, rows), 1)
    tri = ((fi >= fj) & (fi // chunk == fj // chunk)).astype(BF16)
    cum = None
    rem = ld
    for _ in range(3):
        part = rem.astype(BF16)
        rem = rem - part.astype(F32)
        term = _dot(tri, part)
        cum = term if cum is None else cum + term
    tot_seq = [cum[(bi + 1) * chunk - 1:(bi + 1) * chunk] for bi in seqs]
    tot = jnp.concatenate([jnp.broadcast_to(t, (chunk, RW_WIDTH)) for t in tot_seq], axis=0)
    kkt = (kk * jnp.exp(cum - ld)).astype(BF16)
    rt = (r * jnp.exp(cum)).astype(BF16)
    pinv = jnp.exp(-cum)
    kh = (kmod * pinv).astype(BF16)
    bh = (b * pinv).astype(BF16)
    pend = jnp.exp(tot - cum)
    kp = (kmod * pend).astype(BF16)
    bp = (b * pend).astype(BF16)
    ptot = [jnp.exp(t) for t in tot_seq]
    vb = v.astype(BF16)

    chains = [(bi, hd) for bi in seqs for hd in range(RW_HEADS)]
    heads = range(len(chains))
    rsl = [slice(bi * chunk, (bi + 1) * chunk) for bi, _ in chains]
    hsl = [slice(hd * HEAD_DIM, (hd + 1) * HEAD_DIM) for _, hd in chains]
    s_old = [s_scr[bi, hd] for bi, hd in chains]
    v_h = [vb[rsl[ch], hsl[ch]] for ch in heads]
    lhs = [jnp.concatenate([kkt[rsl[ch], hsl[ch]], rt[rsl[ch], hsl[ch]]], axis=0) for ch in heads]
    rhs = [jnp.concatenate([kh[rsl[ch], hsl[ch]], bh[rsl[ch], hsl[ch]]], axis=0) for ch in heads]
    g = [_dot(lhs[hd], rhs[hd], NT) for hd in heads]
    from_state = [_dot(lhs[hd], s_old[hd].astype(BF16), NT) for hd in heads]
    a_k = [jnp.where(strict, g[hd][:chunk, :chunk], 0.0).astype(BF16) for hd in heads]
    a_rk = [jnp.where(incl, g[hd][chunk:, :chunk], 0.0).astype(BF16) for hd in heads]
    a_rb = [jnp.where(incl, g[hd][chunk:, chunk:], 0.0).astype(BF16) for hd in heads]
    pw = [jnp.where(strict, -g[hd][:chunk, chunk:], 0.0) for hd in heads]
    tinv = [eye + pw[hd] for hd in heads]
    for _ in range(int(math.log2(chunk)) - 1):
        pwb = [pw[hd].astype(BF16) for hd in heads]
        pw = [_dot(pwb[hd], pwb[hd]) for hd in heads]
        tinv = [tinv[hd] + _dot(tinv[hd].astype(BF16), pw[hd].astype(BF16)) for hd in heads]
    rhs_u = [from_state[hd][:chunk] + _dot(a_k[hd], v_h[hd]) for hd in heads]
    y_part = [from_state[hd][chunk:] + _dot(a_rk[hd], v_h[hd]) for hd in heads]
    s_part = [s_old[hd] * ptot[chains[hd][0]][:, hsl[hd]] + _dot(v_h[hd], kp[rsl[hd], hsl[hd]], TN) for hd in heads]
    u = [_dot(tinv[hd].astype(BF16), rhs_u[hd].astype(BF16)).astype(BF16) for hd in heads]
    ys = [y_part[hd] - _dot(a_rb[hd], u[hd]) for hd in heads]
    for ch, (bi, hd) in enumerate(chains):
        s_scr[bi, hd] = s_part[ch] - _dot(u[ch], bp[rsl[ch], hsl[ch]], TN)
    y = jnp.concatenate([jnp.concatenate(ys[bi * RW_HEADS:(bi + 1) * RW_HEADS], axis=1) for bi in seqs],
                        axis=0)

    inv_n = 1.0 / HEAD_DIM
    sums = _seg_sum(jnp.concatenate([y, r * kmod * rk_ref[...]], axis=0), seg, seg_t)
    yc = y - sums[:rows] * inv_n
    var = _seg_sum(yc * yc, seg, seg_t) * inv_n
    y = yc * lax.rsqrt(var + RW_GN_EPS) * gnw_ref[...] + gnb_ref[...]
    y = y + sums[rows:] * v
    for bi in seqs:
        y_ref[bi] = y[bi * chunk:(bi + 1) * chunk]

    @pl.when(c == pl.num_programs(1) - 1)
    def _():
        sout_ref[...] = s_scr[...]


RW_SEQS_PER_STEP = 4


def _rwkv(z3, zlast, s0, params, chunk, n_valid):
    bsz, t, _ = z3.shape
    nc = t // chunk
    nb = RW_SEQS_PER_STEP
    assert bsz % nb == 0
    full = lambda a: pl.BlockSpec(a.shape, lambda bi, ci: (0,) * a.ndim)
    mu, w0, wup, a0, aup, k_k, k_a, r_k, gn_w, gn_b = params
    ops = [mu.reshape(1, -1), w0.reshape(1, -1), wup.astype(BF16), a0.reshape(1, -1), aup.astype(BF16),
           k_k.reshape(1, -1), k_a.reshape(1, -1), r_k.reshape(1, -1), gn_w.reshape(1, -1), gn_b.reshape(1, -1)]
    kern = functools.partial(_rwkv_kernel, chunk=chunk, n_valid=n_valid, nb=nb)
    state_spec = pl.BlockSpec((nb, RW_HEADS, HEAD_DIM, HEAD_DIM), lambda bi, ci: (bi, 0, 0, 0))
    return pl.pallas_call(
        kern,
        grid=(bsz // nb, nc),
        in_specs=[pl.BlockSpec((nb, chunk, RW_SHIFT_WIDTH), lambda bi, ci: (bi, ci, 0)),
                  pl.BlockSpec((nb, 1, RW_SHIFT_WIDTH), lambda bi, ci: (bi, 0, 0)),
                  state_spec]
                 + [full(o) for o in ops],
        out_specs=[pl.BlockSpec((nb, chunk, RW_WIDTH), lambda bi, ci: (bi, ci, 0)), state_spec],
        out_shape=[jax.ShapeDtypeStruct((bsz, t, RW_WIDTH), F32),
                   jax.ShapeDtypeStruct((bsz, RW_HEADS, HEAD_DIM, HEAD_DIM), F32)],
        scratch_shapes=[pltpu.VMEM((nb, RW_HEADS, HEAD_DIM, HEAD_DIM), F32),
                        pltpu.VMEM((nb, 1, RW_SHIFT_WIDTH), F32)],
        compiler_params=pltpu.CompilerParams(dimension_semantics=("parallel", "arbitrary"),
                                             vmem_limit_bytes=VMEM_LIMIT_V7X),
        name="rwkv",
    )(z3, zlast.reshape(bsz, 1, RW_SHIFT_WIDTH), s0, *ops)


def _bias_kernel(rel_ref, o_ref):
    hd = pl.program_id(0)
    kj = lax.broadcasted_iota(jnp.int32, (MOBA_BLOCK, MOBA_BLOCK), 0)
    qi = lax.broadcasted_iota(jnp.int32, (MOBA_BLOCK, MOBA_BLOCK), 1)
    exact = REL_BUCKETS // 2
    for which in range(2):
        d = jnp.maximum(qi - kj + which * MOBA_BLOCK, 0)
        logd = jnp.log(jnp.maximum(d, 1).astype(F32) / exact) / math.log(REL_MAX_DIST / exact)
        large = jnp.minimum(exact + (logd * (REL_BUCKETS - exact)).astype(jnp.int32), REL_BUCKETS - 1)
        bucket = jnp.where(d < exact, d, large)
        acc = jnp.zeros((MOBA_BLOCK, MOBA_BLOCK), F32)
        for bkt in range(REL_BUCKETS):
            acc = jnp.where(bucket == bkt, rel_ref[bkt, hd], acc)
        acc = acc * LOG2E
        if which == 0:
            acc = jnp.where(kj <= qi, acc, NEG_BIG)
        o_ref[0, which] = acc
    o_ref[0, 2] = jnp.full((MOBA_BLOCK, MOBA_BLOCK), rel_ref[REL_BUCKETS - 1, hd], F32) * LOG2E


N_BIAS = 3


def _bias_tables(rel_table):
    return pl.pallas_call(
        _bias_kernel,
        grid=(AT_HEADS,),
        in_specs=[pl.BlockSpec(memory_space=pltpu.SMEM)],
        out_specs=pl.BlockSpec((1, N_BIAS, MOBA_BLOCK, MOBA_BLOCK), lambda h: (h, 0, 0, 0)),
        out_shape=jax.ShapeDtypeStruct((AT_HEADS, N_BIAS, MOBA_BLOCK, MOBA_BLOCK), F32),
        name="rel_bias",
    )(rel_table)


def _topk_mask(gate, valid, blk, nblk):
    rank = jnp.zeros(gate.shape, jnp.int32)
    for jp in range(nblk):
        gj = gate[jp:jp + 1, :]
        beats = (gj > gate) | ((gj == gate) & (jp < blk))
        rank = rank + beats.astype(jnp.int32)
    return valid & (rank < MOBA_TOPK)


MOBA_HEADS_PER_STEP = 8
ONES_ROWS = 16


def _moba_kernel(q_ref, k_ref, vt_ref, bias_ref, o_ref, kmean_scr, mask_scr, *, nblk):
    i = pl.program_id(2)
    heads = range(MOBA_HEADS_PER_STEP)
    pair = 2 * MOBA_BLOCK

    @pl.when(i == 0)
    def _():
        kf = k_ref[0].astype(F32).reshape(nblk, MOBA_BLOCK, MOBA_HEADS_PER_STEP * HEAD_DIM)
        kmean_scr[...] = jnp.sum(kf, axis=1) * (1.0 / MOBA_BLOCK)

    blk = lax.broadcasted_iota(jnp.int32, (nblk, MOBA_BLOCK), 0)
    q2 = q_ref[0].astype(F32)
    hsl = [slice(hh * HEAD_DIM, (hh + 1) * HEAD_DIM) for hh in heads]
    valid = blk < i
    qst = []
    for hh in heads:
        q = q2[:, hsl[hh]]
        gate = _dot(kmean_scr[:, hsl[hh]], q, NT, HIGHEST)
        gate = jnp.where(valid, gate, -jnp.inf)
        keep = _topk_mask(gate, valid, blk, nblk) | (blk == i)
        mask_scr[hh] = jnp.where(keep, 0.0, NEG_BIG)
        qst.append(jnp.transpose(q).astype(BF16))
    ones_rows = jnp.ones((ONES_ROWS, pair), BF16)

    def body(step, carry):
        pp = i // 2 - step
        r0 = pl.multiple_of(pp * pair, pair)
        j_lo = 2 * pp
        out = []
        s_parts = []
        for hh in heads:
            s = _dot(k_ref[0, pl.ds(r0, pair), hsl[hh]], qst[hh])
            parts = []
            for half in range(2):
                j = j_lo + half
                which = jnp.clip(i - j, 0, N_BIAS - 1)
                parts.append(s[half * MOBA_BLOCK:(half + 1) * MOBA_BLOCK] + bias_ref[hh, which]
                             + mask_scr[hh, pl.ds(j, 1), :])
            s_parts.append(parts)
        m_new = [jnp.maximum(carry[hh][0],
                             jnp.max(jnp.maximum(s_parts[hh][0], s_parts[hh][1]), axis=0, keepdims=True))
                 for hh in heads]
        pr = [jnp.concatenate([jnp.exp2(part - m_new[hh]) for part in s_parts[hh]], axis=0).astype(BF16)
              for hh in heads]
        pv = [_dot(jnp.concatenate([vt_ref[0, hsl[hh], pl.ds(r0, pair)], ones_rows], axis=0), pr[hh])
              for hh in heads]
        for hh in heads:
            m, acc = carry[hh]
            out.append((m_new[hh], jnp.exp2(m - m_new[hh]) * acc + pv[hh]))
        return tuple(out)

    init = tuple((jnp.full((1, MOBA_BLOCK), NEG_BIG, F32),
                  jnp.zeros((HEAD_DIM + ONES_ROWS, MOBA_BLOCK), F32)) for _ in heads)
    res = lax.fori_loop(0, i // 2 + 1, body, init)
    o_ref[0] = jnp.concatenate(
        [jnp.transpose(acc[:HEAD_DIM] / acc[HEAD_DIM:HEAD_DIM + 1]) for (_, acc) in res], axis=1)


def _moba_prompt(qb, kb, vtb, bias):
    bsz, t, _ = qb.shape
    nblk = t // MOBA_BLOCK
    assert nblk % 2 == 0
    hw = MOBA_HEADS_PER_STEP * HEAD_DIM
    kern = functools.partial(_moba_kernel, nblk=nblk)
    return pl.pallas_call(
        kern,
        grid=(bsz, AT_HEADS // MOBA_HEADS_PER_STEP, nblk),
        in_specs=[pl.BlockSpec((1, MOBA_BLOCK, hw), lambda b, h, i: (b, i, h)),
                  pl.BlockSpec((1, t, hw), lambda b, h, i: (b, 0, h)),
                  pl.BlockSpec((1, hw, t), lambda b, h, i: (b, h, 0)),
                  pl.BlockSpec((MOBA_HEADS_PER_STEP, N_BIAS, MOBA_BLOCK, MOBA_BLOCK), lambda b, h, i: (h, 0, 0, 0))],
        out_specs=pl.BlockSpec((1, MOBA_BLOCK, hw), lambda b, h, i: (b, i, h)),
        out_shape=jax.ShapeDtypeStruct((bsz, t, AT_WIDTH), F32),
        scratch_shapes=[pltpu.VMEM((nblk, hw), F32),
                        pltpu.VMEM((MOBA_HEADS_PER_STEP, nblk, MOBA_BLOCK), F32)],
        compiler_params=pltpu.CompilerParams(dimension_semantics=("parallel", "parallel", "arbitrary"),
                                             vmem_limit_bytes=VMEM_LIMIT_V7X),
        name="moba_prompt",
    )(qb, kb, vtb, bias)


DEC_Q = 4
DEC_ROWS = DEC_Q * AT_HEADS
DEC_PAD = 8
PAGES_PER_BLOCK = MOBA_BLOCK // PAGE_SIZE
DEC_BLOCKS_PER_STEP = 4
DEC_PAGES_PER_STEP = DEC_BLOCKS_PER_STEP * PAGES_PER_BLOCK


def _moba_decode_kernel(pt_ref, q_ref, knew_ref, vnew_ref, *rest, nblk):
    del pt_ref
    k_refs = rest[:DEC_PAGES_PER_STEP]
    v_refs = rest[DEC_PAGES_PER_STEP:2 * DEC_PAGES_PER_STEP]
    bias_ref, bown_ref, o_ref, qbd_scr, gate_scr, m_scr, l_scr, acc_scr = rest[2 * DEC_PAGES_PER_STEP:]
    step = pl.program_id(1)
    col = lax.broadcasted_iota(jnp.int32, (DEC_ROWS, LANES), 1)
    own_head = (lax.broadcasted_iota(jnp.int32, (DEC_ROWS, AT_WIDTH), 1) // HEAD_DIM
                == lax.broadcasted_iota(jnp.int32, (DEC_ROWS, AT_WIDTH), 0) % AT_HEADS)

    @pl.when(step == 0)
    def _():
        q = q_ref[0]
        rows = [jnp.broadcast_to(q[qq:qq + 1], (AT_HEADS, AT_WIDTH)) for qq in range(DEC_Q)]
        qbd_scr[...] = jnp.where(own_head, jnp.concatenate(rows, axis=0), jnp.zeros((), BF16))
        gate_scr[...] = jnp.full((DEC_ROWS, LANES), -jnp.inf, F32)
        m_scr[...] = jnp.full((DEC_ROWS, LANES), NEG_BIG, F32)
        l_scr[...] = jnp.zeros((DEC_ROWS, LANES), F32)

    qbd = qbd_scr[...]

    def put(ref, column, value):
        ref[...] = jnp.where(col == column, value, ref[...])

    def softmax_partial(j, s):
        m = jnp.max(s, axis=1, keepdims=True)
        pr = jnp.exp2(s - m)
        put(m_scr, j, m)
        put(l_scr, j, jnp.sum(pr, axis=1, keepdims=True))
        return pr.astype(BF16)

    blocks = range(DEC_BLOCKS_PER_STEP)
    js = [step * DEC_BLOCKS_PER_STEP + blk for blk in blocks]
    page_view = lambda ref: ref[0].reshape(AT_WIDTH, PAGE_SIZE).astype(BF16)
    raw = [jnp.concatenate([_dot(qbd, page_view(k_refs[blk * PAGES_PER_BLOCK + pg]))
                            for pg in range(PAGES_PER_BLOCK)], axis=1) for blk in blocks]
    weights = []
    for blk in blocks:
        put(gate_scr, js[blk], jnp.sum(raw[blk], axis=1, keepdims=True))
        which = (js[blk] == nblk - 1).astype(jnp.int32)
        weights.append(softmax_partial(js[blk], raw[blk] + bias_ref[which]))
    for blk in blocks:
        acc_scr[js[blk]] = sum(_dot(weights[blk][:, pg * PAGE_SIZE:(pg + 1) * PAGE_SIZE],
                                    page_view(v_refs[blk * PAGES_PER_BLOCK + pg]), NT)
                               for pg in range(PAGES_PER_BLOCK))

    @pl.when(step == pl.num_programs(1) - 1)
    def _():
        knew = knew_ref[0].astype(BF16)
        vnew = vnew_ref[0].astype(BF16)
        acc_scr[nblk] = _dot(softmax_partial(nblk, _dot(qbd, knew, NT) + bown_ref[...]), vnew)
        gates = gate_scr[...]
        valid = col < nblk
        rank = jnp.zeros((DEC_ROWS, LANES), jnp.int32)
        for jp in range(nblk):
            gj = gates[:, jp:jp + 1]
            rank = rank + ((gj > gates) | ((gj == gates) & (jp < col))).astype(jnp.int32)
        sel = (valid & (rank < MOBA_TOPK)) | (col == nblk)
        m_all = m_scr[...]
        m_top = jnp.max(jnp.where(sel, m_all, NEG_BIG), axis=1, keepdims=True)
        wgt = jnp.where(sel, jnp.exp2(m_all - m_top), 0.0)
        denom = jnp.sum(wgt * l_scr[...], axis=1, keepdims=True)
        total = jnp.zeros((DEC_ROWS, AT_WIDTH), F32)
        for jj in range(nblk + 1):
            total = total + wgt[:, jj:jj + 1] * acc_scr[jj]
        out = jnp.where(own_head, total / denom, 0.0)
        o_ref[0] = jnp.sum(out.reshape(DEC_Q, AT_HEADS, AT_WIDTH), axis=1)


def _moba_decode(page_table, qb, k_new, v_new, cache_k, cache_v, bias2, bown):
    bsz = qb.shape[0]
    n_pages = page_table.shape[1]
    nblk = n_pages // PAGES_PER_BLOCK
    assert nblk % DEC_BLOCKS_PER_STEP == 0 and nblk + 1 <= LANES
    kt = jnp.transpose(cache_k, (0, 2, 3, 1))
    vt = jnp.transpose(cache_v, (0, 2, 3, 1))
    page = lambda off: pl.BlockSpec((1, AT_HEADS, HEAD_DIM, PAGE_SIZE),
                                    lambda b, s, pt: (pt[b, DEC_PAGES_PER_STEP * s + off], 0, 0, 0))
    per_b = lambda rows: pl.BlockSpec((1, rows, AT_WIDTH), lambda b, s, pt: (b, 0, 0))
    full = lambda a: pl.BlockSpec(a.shape, lambda b, s, pt: (0,) * a.ndim)
    pages = [page(off) for off in range(DEC_PAGES_PER_STEP)]
    kern = functools.partial(_moba_decode_kernel, nblk=nblk)
    grid_spec = pltpu.PrefetchScalarGridSpec(
        num_scalar_prefetch=1,
        grid=(bsz, nblk // DEC_BLOCKS_PER_STEP),
        in_specs=[per_b(DEC_Q), per_b(DEC_PAD), per_b(DEC_PAD)] + pages + pages + [full(bias2), full(bown)],
        out_specs=per_b(DEC_Q),
        scratch_shapes=[pltpu.VMEM((DEC_ROWS, AT_WIDTH), BF16),
                        pltpu.VMEM((DEC_ROWS, LANES), F32),
                        pltpu.VMEM((DEC_ROWS, LANES), F32),
                        pltpu.VMEM((DEC_ROWS, LANES), F32),
                        pltpu.VMEM((nblk + 1, DEC_ROWS, AT_WIDTH), F32)],
    )
    return pl.pallas_call(
        kern,
        grid_spec=grid_spec,
        out_shape=jax.ShapeDtypeStruct((bsz, DEC_Q, AT_WIDTH), F32),
        compiler_params=pltpu.CompilerParams(dimension_semantics=("parallel", "arbitrary"),
                                             vmem_limit_bytes=VMEM_LIMIT_V7X),
        name="moba_decode",
    )(page_table, qb, k_new, v_new, *([kt] * DEC_PAGES_PER_STEP), *([vt] * DEC_PAGES_PER_STEP), bias2, bown)


def _decode_bias(bias):
    last = jnp.transpose(bias[:, 1, :, :DEC_Q], (2, 0, 1)).reshape(DEC_ROWS, MOBA_BLOCK)
    far = jnp.broadcast_to(bias[:, 2, 0, 0][None, :, None], (DEC_Q, AT_HEADS, MOBA_BLOCK)).reshape(DEC_ROWS, MOBA_BLOCK)
    own = jnp.transpose(bias[:, 0, :DEC_PAD, :DEC_Q], (2, 0, 1)).reshape(DEC_ROWS, DEC_PAD)
    return jnp.stack([far, last]), own


def _out_proj_kernel(x_ref, yrw_ref, grw_ref, yat_ref, gat_ref, m_ref, worw_ref, woat_ref, wout_ref, o_ref):
    o_rw = _dot((yrw_ref[...] * _silu(grw_ref[...].astype(F32))).astype(BF16), worw_ref[...])
    o_at = _dot((yat_ref[...] * _silu(gat_ref[...].astype(F32))).astype(BF16), woat_ref[...])
    m = m_ref[...].astype(F32)
    merged = _sigmoid(m[:, :D_MODEL]) * o_rw + _sigmoid(m[:, D_MODEL:]) * o_at
    o_ref[...] = x_ref[...] + _dot(merged.astype(BF16), wout_ref[...])


def _out_proj(x2d, y_rw, g_rw, y_at, g_at, m, w_o_rwkv, w_o_attn, w_out, tm):
    n = x2d.shape[0]
    row = lambda w: pl.BlockSpec((tm, w), lambda i: (i, 0))
    full = lambda a: pl.BlockSpec(a.shape, lambda i: (0,) * a.ndim)
    ws = [w_o_rwkv.astype(BF16), w_o_attn.astype(BF16), w_out.astype(BF16)]
    return pl.pallas_call(
        _out_proj_kernel,
        grid=(n // tm,),
        in_specs=[row(D_MODEL), row(RW_WIDTH), row(RW_WIDTH), row(AT_WIDTH), row(AT_WIDTH), row(2 * D_MODEL)]
                 + [full(w) for w in ws],
        out_specs=row(D_MODEL),
        out_shape=jax.ShapeDtypeStruct((n, D_MODEL), F32),
        compiler_params=pltpu.CompilerParams(dimension_semantics=("parallel",),
                                             vmem_limit_bytes=VMEM_LIMIT_V7X),
        name="out_proj",
    )(x2d, y_rw, g_rw, y_at, g_at, m, *ws)


def _row_tile(n):
    return 256 if n % 256 == 0 else n


def kernel(x_prompt, x_sample, cache_k, cache_v, page_table, state_shift, state_wkv, rel_table, norm_g, w_in, rw_mu, rw_w0, rw_w_up, rw_a0, rw_a_up, rw_k_k, rw_k_a, rw_r_k, rw_gn_w, rw_gn_b, at_q_norm, at_k_norm, w_o_rwkv, w_o_attn, w_out):
    depth = w_in.shape[0]
    assert depth == 1
    l = 0
    bp, tp, _ = x_prompt.shape
    bs, ts, _ = x_sample.shape
    assert ts == DEC_Q
    rw_params = (rw_mu[l], rw_w0[l], rw_w_up[l], rw_a0[l], rw_a_up[l], rw_k_k[l], rw_k_a[l], rw_r_k[l],
                 rw_gn_w[l], rw_gn_b[l])
    w_in_b = w_in[l].astype(BF16)
    bias = _bias_tables(rel_table)

    xp = x_prompt.reshape(bp * tp, D_MODEL)
    zrw, grw, gat, m, kt, vt, qb, kb, vtb = _in_proj(xp, norm_g[l], w_in_b, at_q_norm[l], at_k_norm[l],
                                                     MOBA_BLOCK, seq_len=tp)
    zrw3 = zrw.reshape(bp, tp, RW_SHIFT_WIDTH)
    y_rw, wkv_p = _rwkv(zrw3, jnp.zeros((bp, RW_SHIFT_WIDTH), F32),
                        jnp.zeros((bp, RW_HEADS, HEAD_DIM, HEAD_DIM), F32), rw_params, 64, 64)
    y_at = _moba_prompt(qb.reshape(bp, tp, AT_WIDTH), kb.reshape(bp, tp, AT_WIDTH), vtb, bias)
    y_p = _out_proj(xp, y_rw.reshape(bp * tp, RW_WIDTH), grw, y_at.reshape(bp * tp, AT_WIDTH), gat, m,
                    w_o_rwkv[l], w_o_attn[l], w_out[l], _row_tile(bp * tp)).reshape(bp, tp, D_MODEL)
    k_p = jnp.transpose(kt.reshape(bp, AT_HEADS, HEAD_DIM, tp), (0, 3, 1, 2))[None]
    v_p = jnp.transpose(vt.reshape(bp, AT_HEADS, HEAD_DIM, tp), (0, 3, 1, 2))[None]
    shift_p = zrw3[:, tp - 1][None]

    xs = x_sample.reshape(bs * ts, D_MODEL)
    zrw, grw, gat, m, k, v, qb, kb, vb = _in_proj(xs, norm_g[l], w_in_b, at_q_norm[l], at_k_norm[l], _row_tile(bs * ts))
    zrw3 = zrw.reshape(bs, ts, RW_SHIFT_WIDTH)
    zpad = jnp.pad(zrw3, ((0, 0), (0, DEC_PAD - ts), (0, 0)))
    y_rw, wkv_s = _rwkv(zpad, state_shift[l], state_wkv[l], rw_params, DEC_PAD, ts)
    y_rw = y_rw[:, :ts]
    pad_rows = lambda a: jnp.pad(a.reshape(bs, ts, AT_WIDTH), ((0, 0), (0, DEC_PAD - ts), (0, 0)))
    bias2, bown = _decode_bias(bias)
    y_at = _moba_decode(page_table, qb.reshape(bs, ts, AT_WIDTH), pad_rows(k), pad_rows(v),
                        cache_k[l], cache_v[l], bias2, bown)
    y_s = _out_proj(xs, y_rw.reshape(bs * ts, RW_WIDTH), grw, y_at.reshape(bs * ts, AT_WIDTH), gat, m,
                    w_o_rwkv[l], w_o_attn[l], w_out[l], _row_tile(bs * ts)).reshape(bs, ts, D_MODEL)
    k_s = k.reshape(1, bs, ts, AT_HEADS, HEAD_DIM)
    v_s = v.reshape(1, bs, ts, AT_HEADS, HEAD_DIM)
    shift_s = zrw3[:, ts - 1][None]

    return (y_p, y_s, k_p, v_p, shift_p, wkv_p[None], k_s, v_s, shift_s, wkv_s[None])
```

```python
import functools
import math

import jax
import jax.numpy as jnp
from jax import lax
from jax.experimental import pallas as pl
from jax.experimental.pallas import tpu as pltpu

D_MODEL = 1024
HEAD_DIM = 64
RW_WIDTH = 512
RW_HEADS = 8
RW_RANK = 64
RW_SHIFT_WIDTH = 3 * RW_WIDTH + 2 * RW_RANK
RW_GN_EPS = 64e-5
AT_WIDTH = 512
AT_HEADS = 8
MOBA_BLOCK = 256
MOBA_TOPK = 3
PAGE_SIZE = 128
REL_BUCKETS = 32
REL_MAX_DIST = 128
NORM_EPS = 1e-6
NEG_BIG = -1e30
IN_WIDTH = RW_SHIFT_WIDTH + RW_WIDTH + 4 * AT_WIDTH + 2 * D_MODEL
C_GRW = RW_SHIFT_WIDTH
C_Q = C_GRW + RW_WIDTH
C_K = C_Q + AT_WIDTH
C_V = C_K + AT_WIDTH
C_GAT = C_V + AT_WIDTH
C_M = C_GAT + AT_WIDTH

VMEM_LIMIT_V7X = 56 * 1024 * 1024
LANES = 128
LOG2E = math.log2(math.e)
QK_SCALE = HEAD_DIM ** -0.5 * LOG2E
F32 = jnp.float32
BF16 = jnp.bfloat16
HIGHEST = lax.Precision.HIGHEST

NN = (((1,), (0,)), ((), ()))
NT = (((1,), (1,)), ((), ()))
TN = (((0,), (0,)), ((), ()))


def _dot(a, b, dims=NN, precision=None):
    return lax.dot_general(a, b, dims, precision=precision, preferred_element_type=F32)


def _split_dot(x, ones_bf16, passes):
    acc = None
    rem = x
    for _ in range(passes):
        part = rem.astype(BF16)
        rem = rem - part.astype(F32)
        term = _dot(part, ones_bf16)
        acc = term if acc is None else acc + term
    return acc


def _sigmoid(x):
    return 1.0 / (1.0 + jnp.exp(-x))


def _silu(x):
    return x * _sigmoid(x)


def _head_segments():
    seg = (lax.broadcasted_iota(jnp.int32, (RW_WIDTH, LANES), 0) // HEAD_DIM
           == lax.broadcasted_iota(jnp.int32, (RW_WIDTH, LANES), 1)).astype(BF16)
    seg_t = (lax.broadcasted_iota(jnp.int32, (LANES, RW_WIDTH), 1) // HEAD_DIM
             == lax.broadcasted_iota(jnp.int32, (LANES, RW_WIDTH), 0)).astype(BF16)
    return seg, seg_t


def _seg_sum(x, seg, seg_t):
    return _split_dot(_split_dot(x, seg, 3), seg_t, 3)


def _in_proj_kernel(x_ref, g_ref, w_ref, qn_ref, kn_ref,
                    zrw_ref, grw_ref, gat_ref, m_ref, k_ref, v_ref, qb_ref, kb_ref, vb_ref, *, token_minor):
    x = x_ref[...]
    ms = jnp.mean(x * x, axis=-1, keepdims=True)
    h = (x * lax.rsqrt(ms + NORM_EPS) * g_ref[...]).astype(BF16)
    seg, _ = _head_segments()

    def proj(lo, hi):
        return _dot(h, w_ref[:, lo:hi])

    def head_norm(z, gain):
        sums = _split_dot(z * z, seg, 2)
        msq = jnp.concatenate([jnp.broadcast_to(sums[:, hd:hd + 1], (z.shape[0], HEAD_DIM))
                               for hd in range(AT_HEADS)], axis=1) * (1.0 / HEAD_DIM)
        return z * lax.rsqrt(msq + NORM_EPS) * gain

    zrw_ref[...] = proj(0, C_GRW)
    grw_ref[...] = proj(C_GRW, C_Q).astype(BF16)
    q = head_norm(proj(C_Q, C_K), qn_ref[...])
    qb_ref[...] = (q * QK_SCALE).astype(BF16)
    k = head_norm(proj(C_K, C_V), kn_ref[...])
    kb_ref[...] = k.astype(BF16)
    v = proj(C_V, C_GAT)
    if token_minor:
        k_ref[0] = jnp.transpose(k)
        vt = jnp.transpose(v)
        v_ref[0] = vt
        vb_ref[0] = vt.astype(BF16)
    else:
        k_ref[...] = k
        v_ref[...] = v
        vb_ref[...] = v.astype(BF16)
    gat_ref[...] = proj(C_GAT, C_M).astype(BF16)
    m_ref[...] = proj(C_M, IN_WIDTH).astype(BF16)


def _in_proj(x2d, norm_g, w_bf16, q_norm, k_norm, tm, seq_len=None):
    n = x2d.shape[0]
    row = lambda w: pl.BlockSpec((tm, w), lambda i: (i, 0))
    full = lambda a: pl.BlockSpec(a.shape, lambda i: (0,) * a.ndim)
    g2 = norm_g.reshape(1, D_MODEL)
    qn = jnp.tile(q_norm, AT_HEADS).reshape(1, AT_WIDTH)
    kn = jnp.tile(k_norm, AT_HEADS).reshape(1, AT_WIDTH)
    widths = (RW_SHIFT_WIDTH, RW_WIDTH, AT_WIDTH, 2 * D_MODEL)
    if seq_len is None:
        kv_shape, kv_spec = (n, AT_WIDTH), row(AT_WIDTH)
    else:
        tiles = seq_len // tm
        assert seq_len % tm == 0
        kv_shape = (n // seq_len, AT_WIDTH, seq_len)
        kv_spec = pl.BlockSpec((1, AT_WIDTH, tm), lambda i: (i // tiles, 0, i % tiles))
    out_shape = ([jax.ShapeDtypeStruct((n, w), F32 if idx == 0 else BF16) for idx, w in enumerate(widths)]
                 + [jax.ShapeDtypeStruct(kv_shape, F32)] * 2
                 + [jax.ShapeDtypeStruct((n, AT_WIDTH), BF16)] * 2 + [jax.ShapeDtypeStruct(kv_shape, BF16)])
    out_specs = [row(w) for w in widths] + [kv_spec] * 2 + [row(AT_WIDTH)] * 2 + [kv_spec]
    return pl.pallas_call(
        functools.partial(_in_proj_kernel, token_minor=seq_len is not None),
        grid=(n // tm,),
        in_specs=[row(D_MODEL), full(g2),
                  pl.BlockSpec(w_bf16.shape, lambda i: (0, 0), pipeline_mode=pl.Buffered(1)),
                  full(qn), full(kn)],
        out_specs=out_specs,
        out_shape=out_shape,
        compiler_params=pltpu.CompilerParams(dimension_semantics=("parallel",),
                                             vmem_limit_bytes=VMEM_LIMIT_V7X),
        name="in_proj",
    )(x2d, g2, w_bf16, qn, kn)


def _rwkv_kernel(z_ref, zlast_ref, s0_ref, mu_ref, w0_ref, wup_ref, a0_ref, aup_ref, kk_ref, ka_ref,
                 rk_ref, gnw_ref, gnb_ref, y_ref, sout_ref, s_scr, zprev_scr, *, chunk, n_valid, nb):
    c = pl.program_id(1)

    @pl.when(c == 0)
    def _():
        s_scr[...] = s0_ref[...]
        zprev_scr[...] = zlast_ref[...]

    seqs = range(nb)
    rows = nb * chunk
    row = lax.broadcasted_iota(jnp.int32, (chunk, 1), 0)
    z_seq = [z_ref[bi] for bi in seqs]
    zp = jnp.concatenate([jnp.where(row == 0, zprev_scr[bi], pltpu.roll(z_seq[bi], 1, axis=0)) for bi in seqs],
                         axis=0)
    for bi in seqs:
        zprev_scr[bi] = z_seq[bi][n_valid - 1:n_valid]
    z = jnp.concatenate(z_seq, axis=0)
    row = lax.broadcasted_iota(jnp.int32, (rows, 1), 0) % chunk
    zl = z + (zp - z) * mu_ref[...]
    r = zl[:, 0:RW_WIDTH]
    k = zl[:, RW_WIDTH:2 * RW_WIDTH]
    v = zl[:, 2 * RW_WIDTH:3 * RW_WIDTH]
    wd = zl[:, 3 * RW_WIDTH:3 * RW_WIDTH + RW_RANK]
    ad = zl[:, 3 * RW_WIDTH + RW_RANK:RW_SHIFT_WIDTH]

    x = -(w0_ref[...] + _dot(jnp.tanh(wd).astype(BF16), wup_ref[...]))
    softplus = jnp.maximum(x, 0.0) + jnp.log(1.0 + jnp.exp(-jnp.abs(x)))
    w = -softplus - 0.5
    ld = -jnp.exp(w)
    a = _sigmoid(a0_ref[...] + _dot(ad.astype(BF16), aup_ref[...]))
    seg, seg_t = _head_segments()
    kk = k * kk_ref[...]
    kk = kk / jnp.maximum(jnp.sqrt(_seg_sum(kk * kk, seg, seg_t)), 1e-12)
    kmod = k * (1.0 + (a - 1.0) * ka_ref[...])
    b = kk * a
    if n_valid < chunk:
        ok = row < n_valid
        ld = jnp.where(ok, ld, 0.0)
        kk = jnp.where(ok, kk, 0.0)
        kmod = jnp.where(ok, kmod, 0.0)
        b = jnp.where(ok, b, 0.0)

    ti = lax.broadcasted_iota(jnp.int32, (chunk, chunk), 0)
    tj = lax.broadcasted_iota(jnp.int32, (chunk, chunk), 1)
    incl = ti >= tj
    strict = ti > tj
    eye = (ti == tj).astype(F32)
    fi = lax.broadcasted_iota(jnp.int32, (rows, rows), 0)
    fj = lax.broadcasted_iota(jnp.int32, (rows, rows), 1)
    tri = ((fi >= fj) & (fi // chunk == fj // chunk)).astype(BF16)
    cum = None
    rem = ld
    for _ in range(3):
        part = rem.astype(BF16)
        rem = rem - part.astype(F32)
        term = _dot(tri, part)
        cum = term if cum is None else cum + term
    tot_seq = [cum[(bi + 1) * chunk - 1:(bi + 1) * chunk] for bi in seqs]
    tot = jnp.concatenate([jnp.broadcast_to(t, (chunk, RW_WIDTH)) for t in tot_seq], axis=0)
    kkt = (kk * jnp.exp(cum - ld)).astype(BF16)
    rt = (r * jnp.exp(cum)).astype(BF16)
    pinv = jnp.exp(-cum)
    kh = (kmod * pinv).astype(BF16)
    bh = (b * pinv).astype(BF16)
    pend = jnp.exp(tot - cum)
    kp = (kmod * pend).astype(BF16)
    bp = (b * pend).astype(BF16)
    ptot = [jnp.exp(t) for t in tot_seq]
    vb = v.astype(BF16)

    chains = [(bi, hd) for bi in seqs for hd in range(RW_HEADS)]
    heads = range(len(chains))
    rsl = [slice(bi * chunk, (bi + 1) * chunk) for bi, _ in chains]
    hsl = [slice(hd * HEAD_DIM, (hd + 1) * HEAD_DIM) for _, hd in chains]
    s_old = [s_scr[bi, hd] for bi, hd in chains]
    v_h = [vb[rsl[ch], hsl[ch]] for ch in heads]
    lhs = [jnp.concatenate([kkt[rsl[ch], hsl[ch]], rt[rsl[ch], hsl[ch]]], axis=0) for ch in heads]
    rhs = [jnp.concatenate([kh[rsl[ch], hsl[ch]], bh[rsl[ch], hsl[ch]]], axis=0) for ch in heads]
    g = [_dot(lhs[hd], rhs[hd], NT) for hd in heads]
    from_state = [_dot(lhs[hd], s_old[hd].astype(BF16), NT) for hd in heads]
    a_k = [jnp.where(strict, g[hd][:chunk, :chunk], 0.0).astype(BF16) for hd in heads]
    a_rk = [jnp.where(incl, g[hd][chunk:, :chunk], 0.0).astype(BF16) for hd in heads]
    a_rb = [jnp.where(incl, g[hd][chunk:, chunk:], 0.0).astype(BF16) for hd in heads]
    pw = [jnp.where(strict, -g[hd][:chunk, chunk:], 0.0) for hd in heads]
    tinv = [eye + pw[hd] for hd in heads]
    for _ in range(int(math.log2(chunk)) - 1):
        pwb = [pw[hd].astype(BF16) for hd in heads]
        pw = [_dot(pwb[hd], pwb[hd]) for hd in heads]
        tinv = [tinv[hd] + _dot(tinv[hd].astype(BF16), pw[hd].astype(BF16)) for hd in heads]
    rhs_u = [from_state[hd][:chunk] + _dot(a_k[hd], v_h[hd]) for hd in heads]
    y_part = [from_state[hd][chunk:] + _dot(a_rk[hd], v_h[hd]) for hd in heads]
    s_part = [s_old[hd] * ptot[chains[hd][0]][:, hsl[hd]] + _dot(v_h[hd], kp[rsl[hd], hsl[hd]], TN) for hd in heads]
    u = [_dot(tinv[hd].astype(BF16), rhs_u[hd].astype(BF16)).astype(BF16) for hd in heads]
    ys = [y_part[hd] - _dot(a_rb[hd], u[hd]) for hd in heads]
    for ch, (bi, hd) in enumerate(chains):
        s_scr[bi, hd] = s_part[ch] - _dot(u[ch], bp[rsl[ch], hsl[ch]], TN)
    y = jnp.concatenate([jnp.concatenate(ys[bi * RW_HEADS:(bi + 1) * RW_HEADS], axis=1) for bi in seqs],
                        axis=0)

    inv_n = 1.0 / HEAD_DIM
    sums = _seg_sum(jnp.concatenate([y, r * kmod * rk_ref[...]], axis=0), seg, seg_t)
    yc = y - sums[:rows] * inv_n
    var = _seg_sum(yc * yc, seg, seg_t) * inv_n
    y = yc * lax.rsqrt(var + RW_GN_EPS) * gnw_ref[...] + gnb_ref[...]
    y = y + sums[rows:] * v
    for bi in seqs:
        y_ref[bi] = y[bi * chunk:(bi + 1) * chunk]

    @pl.when(c == pl.num_programs(1) - 1)
    def _():
        sout_ref[...] = s_scr[...]


RW_SEQS_PER_STEP = 4


def _rwkv(z3, zlast, s0, params, chunk, n_valid):
    bsz, t, _ = z3.shape
    nc = t // chunk
    nb = RW_SEQS_PER_STEP
    assert bsz % nb == 0
    full = lambda a: pl.BlockSpec(a.shape, lambda bi, ci: (0,) * a.ndim)
    mu, w0, wup, a0, aup, k_k, k_a, r_k, gn_w, gn_b = params
    ops = [mu.reshape(1, -1), w0.reshape(1, -1), wup.astype(BF16), a0.reshape(1, -1), aup.astype(BF16),
           k_k.reshape(1, -1), k_a.reshape(1, -1), r_k.reshape(1, -1), gn_w.reshape(1, -1), gn_b.reshape(1, -1)]
    kern = functools.partial(_rwkv_kernel, chunk=chunk, n_valid=n_valid, nb=nb)
    state_spec = pl.BlockSpec((nb, RW_HEADS, HEAD_DIM, HEAD_DIM), lambda bi, ci: (bi, 0, 0, 0))
    return pl.pallas_call(
        kern,
        grid=(bsz // nb, nc),
        in_specs=[pl.BlockSpec((nb, chunk, RW_SHIFT_WIDTH), lambda bi, ci: (bi, ci, 0)),
                  pl.BlockSpec((nb, 1, RW_SHIFT_WIDTH), lambda bi, ci: (bi, 0, 0)),
                  state_spec]
                 + [full(o) for o in ops],
        out_specs=[pl.BlockSpec((nb, chunk, RW_WIDTH), lambda bi, ci: (bi, ci, 0)), state_spec],
        out_shape=[jax.ShapeDtypeStruct((bsz, t, RW_WIDTH), F32),
                   jax.ShapeDtypeStruct((bsz, RW_HEADS, HEAD_DIM, HEAD_DIM), F32)],
        scratch_shapes=[pltpu.VMEM((nb, RW_HEADS, HEAD_DIM, HEAD_DIM), F32),
                        pltpu.VMEM((nb, 1, RW_SHIFT_WIDTH), F32)],
        compiler_params=pltpu.CompilerParams(dimension_semantics=("parallel", "arbitrary"),
                                             vmem_limit_bytes=VMEM_LIMIT_V7X),
        name="rwkv",
    )(z3, zlast.reshape(bsz, 1, RW_SHIFT_WIDTH), s0, *ops)


def _bias_kernel(rel_ref, o_ref):
    hd = pl.program_id(0)
    kj = lax.broadcasted_iota(jnp.int32, (MOBA_BLOCK, MOBA_BLOCK), 0)
    qi = lax.broadcasted_iota(jnp.int32, (MOBA_BLOCK, MOBA_BLOCK), 1)
    exact = REL_BUCKETS // 2
    for which in range(2):
        d = jnp.maximum(qi - kj + which * MOBA_BLOCK, 0)
        logd = jnp.log(jnp.maximum(d, 1).astype(F32) / exact) / math.log(REL_MAX_DIST / exact)
        large = jnp.minimum(exact + (logd * (REL_BUCKETS - exact)).astype(jnp.int32), REL_BUCKETS - 1)
        bucket = jnp.where(d < exact, d, large)
        acc = jnp.zeros((MOBA_BLOCK, MOBA_BLOCK), F32)
        for bkt in range(REL_BUCKETS):
            acc = jnp.where(bucket == bkt, rel_ref[bkt, hd], acc)
        acc = acc * LOG2E
        if which == 0:
            acc = jnp.where(kj <= qi, acc, NEG_BIG)
        o_ref[0, which] = acc
    o_ref[0, 2] = jnp.full((MOBA_BLOCK, MOBA_BLOCK), rel_ref[REL_BUCKETS - 1, hd], F32) * LOG2E


N_BIAS = 3


def _bias_tables(rel_table):
    return pl.pallas_call(
        _bias_kernel,
        grid=(AT_HEADS,),
        in_specs=[pl.BlockSpec(memory_space=pltpu.SMEM)],
        out_specs=pl.BlockSpec((1, N_BIAS, MOBA_BLOCK, MOBA_BLOCK), lambda h: (h, 0, 0, 0)),
        out_shape=jax.ShapeDtypeStruct((AT_HEADS, N_BIAS, MOBA_BLOCK, MOBA_BLOCK), F32),
        name="rel_bias",
    )(rel_table)


def _topk_mask(gate, valid, blk, nblk):
    rank = jnp.zeros(gate.shape, jnp.int32)
    for jp in range(nblk):
        gj = gate[jp:jp + 1, :]
        beats = (gj > gate) | ((gj == gate) & (jp < blk))
        rank = rank + beats.astype(jnp.int32)
    return valid & (rank < MOBA_TOPK)


MOBA_HEADS_PER_STEP = 8
ONES_ROWS = 16
MASK_PAD_ROWS = 8


def _moba_kernel(q_ref, k_ref, vt_ref, bias_ref, o_ref, kmean_scr, mask_scr, *, nblk):
    i = pl.program_id(2)
    heads = range(MOBA_HEADS_PER_STEP)
    pair = 2 * MOBA_BLOCK

    @pl.when(i == 0)
    def _():
        kf = k_ref[0].astype(F32).reshape(nblk, MOBA_BLOCK, MOBA_HEADS_PER_STEP * HEAD_DIM)
        kmean_scr[...] = jnp.sum(kf, axis=1) * (1.0 / MOBA_BLOCK)

    blk = lax.broadcasted_iota(jnp.int32, (nblk, MOBA_BLOCK), 0)
    q2 = q_ref[0].astype(F32)
    hsl = [slice(hh * HEAD_DIM, (hh + 1) * HEAD_DIM) for hh in heads]
    valid = blk < i
    qst = []
    for hh in heads:
        q = q2[:, hsl[hh]]
        gate = _dot(kmean_scr[:, hsl[hh]], q, NT, HIGHEST)
        gate = jnp.where(valid, gate, -jnp.inf)
        keep = _topk_mask(gate, valid, blk, nblk) | (blk == i)
        far_bias = jnp.where(blk <= i - 2, bias_ref[hh, N_BIAS - 1, 0:1, :], 0.0)
        mask_scr[hh, 0:nblk] = jnp.where(keep, 0.0, NEG_BIG) + far_bias
        mask_scr[hh, nblk:] = jnp.full((MASK_PAD_ROWS, MOBA_BLOCK), NEG_BIG, F32)
        qst.append(jnp.transpose(q).astype(BF16))
    ones_rows = jnp.ones((ONES_ROWS, pair), BF16)

    def update(carry, r0, addends):
        out = []
        s_parts = []
        for hh in heads:
            s = _dot(k_ref[0, pl.ds(r0, pair), hsl[hh]], qst[hh])
            s_parts.append([s[half * MOBA_BLOCK:(half + 1) * MOBA_BLOCK] + addends[hh][half]
                            for half in range(2)])
        m_new = [jnp.maximum(carry[hh][0],
                             jnp.max(jnp.maximum(s_parts[hh][0], s_parts[hh][1]), axis=0, keepdims=True))
                 for hh in heads]
        pr = [jnp.concatenate([jnp.exp2(part - m_new[hh]) for part in s_parts[hh]], axis=0).astype(BF16)
              for hh in heads]
        pv = [_dot(jnp.concatenate([vt_ref[0, hsl[hh], pl.ds(r0, pair)], ones_rows], axis=0), pr[hh])
              for hh in heads]
        for hh in heads:
            m, acc = carry[hh]
            out.append((m_new[hh], jnp.exp2(m - m_new[hh]) * acc + pv[hh]))
        return tuple(out)

    init = tuple((jnp.full((1, MOBA_BLOCK), NEG_BIG, F32),
                  jnp.zeros((HEAD_DIM + ONES_ROWS, MOBA_BLOCK), F32)) for _ in heads)
    j_near = jnp.maximum(i - 1, 0)
    near = []
    for hh in heads:
        halves = []
        for half in range(2):
            j = j_near + half
            which = jnp.clip(i - j, 0, N_BIAS - 1)
            halves.append(bias_ref[hh, which] + mask_scr[hh, pl.ds(j, 1), :])
        near.append(halves)
    carry = update(init, pl.multiple_of(j_near * MOBA_BLOCK, MOBA_BLOCK), near)

    def far_body(pp, carry):
        j_hi = jnp.where(2 * pp + 1 <= i - 2, 2 * pp + 1, nblk)
        rows = [[mask_scr[hh, pl.ds(j, 1), :] for j in (2 * pp, j_hi)] for hh in heads]
        return update(carry, pl.multiple_of(pp * pair, pair), rows)

    res = lax.fori_loop(0, jnp.maximum(i, 0) // 2, far_body, carry)
    o_ref[0] = jnp.concatenate(
        [jnp.transpose(acc[:HEAD_DIM] / acc[HEAD_DIM:HEAD_DIM + 1]) for (_, acc) in res], axis=1)


def _moba_prompt(qb, kb, vtb, bias):
    bsz, t, _ = qb.shape
    nblk = t // MOBA_BLOCK
    assert nblk % 2 == 0
    hw = MOBA_HEADS_PER_STEP * HEAD_DIM
    kern = functools.partial(_moba_kernel, nblk=nblk)
    return pl.pallas_call(
        kern,
        grid=(bsz, AT_HEADS // MOBA_HEADS_PER_STEP, nblk),
        in_specs=[pl.BlockSpec((1, MOBA_BLOCK, hw), lambda b, h, i: (b, i, h)),
                  pl.BlockSpec((1, t, hw), lambda b, h, i: (b, 0, h)),
                  pl.BlockSpec((1, hw, t), lambda b, h, i: (b, h, 0)),
                  pl.BlockSpec((MOBA_HEADS_PER_STEP, N_BIAS, MOBA_BLOCK, MOBA_BLOCK), lambda b, h, i: (h, 0, 0, 0))],
        out_specs=pl.BlockSpec((1, MOBA_BLOCK, hw), lambda b, h, i: (b, i, h)),
        out_shape=jax.ShapeDtypeStruct((bsz, t, AT_WIDTH), F32),
        scratch_shapes=[pltpu.VMEM((nblk, hw), F32),
                        pltpu.VMEM((MOBA_HEADS_PER_STEP, nblk + MASK_PAD_ROWS, MOBA_BLOCK), F32)],
        compiler_params=pltpu.CompilerParams(dimension_semantics=("parallel", "parallel", "arbitrary"),
                                             vmem_limit_bytes=VMEM_LIMIT_V7X),
        name="moba_prompt",
    )(qb, kb, vtb, bias)


DEC_Q = 4
DEC_ROWS = DEC_Q * AT_HEADS
DEC_PAD = 8
PAGES_PER_BLOCK = MOBA_BLOCK // PAGE_SIZE
DEC_BLOCKS_PER_STEP = 8
DEC_PAGES_PER_STEP = DEC_BLOCKS_PER_STEP * PAGES_PER_BLOCK


def _moba_decode_kernel(pt_ref, q_ref, knew_ref, vnew_ref, *rest, nblk):
    del pt_ref
    k_refs = rest[:DEC_PAGES_PER_STEP]
    v_refs = rest[DEC_PAGES_PER_STEP:2 * DEC_PAGES_PER_STEP]
    bias_ref, bown_ref, o_ref, qbd_scr, gate_scr, m_scr, l_scr, acc_scr = rest[2 * DEC_PAGES_PER_STEP:]
    step = pl.program_id(1)
    col = lax.broadcasted_iota(jnp.int32, (DEC_ROWS, LANES), 1)
    own_head = (lax.broadcasted_iota(jnp.int32, (DEC_ROWS, AT_WIDTH), 1) // HEAD_DIM
                == lax.broadcasted_iota(jnp.int32, (DEC_ROWS, AT_WIDTH), 0) % AT_HEADS)

    @pl.when(step == 0)
    def _():
        q = q_ref[0]
        rows = [jnp.broadcast_to(q[qq:qq + 1], (AT_HEADS, AT_WIDTH)) for qq in range(DEC_Q)]
        qbd_scr[...] = jnp.where(own_head, jnp.concatenate(rows, axis=0), jnp.zeros((), BF16))
        gate_scr[...] = jnp.full((DEC_ROWS, LANES), -jnp.inf, F32)
        m_scr[...] = jnp.full((DEC_ROWS, LANES), NEG_BIG, F32)
        l_scr[...] = jnp.zeros((DEC_ROWS, LANES), F32)

    qbd = qbd_scr[...]

    def put(ref, column, value):
        ref[...] = jnp.where(col == column, value, ref[...])

    def softmax_partial(j, s):
        m = jnp.max(s, axis=1, keepdims=True)
        pr = jnp.exp2(s - m)
        put(m_scr, j, m)
        put(l_scr, j, jnp.sum(pr, axis=1, keepdims=True))
        return pr.astype(BF16)

    blocks = range(DEC_BLOCKS_PER_STEP)
    js = [step * DEC_BLOCKS_PER_STEP + blk for blk in blocks]
    page_view = lambda ref: ref[0].reshape(AT_WIDTH, PAGE_SIZE).astype(BF16)
    raw = [jnp.concatenate([_dot(qbd, page_view(k_refs[blk * PAGES_PER_BLOCK + pg]))
                            for pg in range(PAGES_PER_BLOCK)], axis=1) for blk in blocks]
    weights = []
    for blk in blocks:
        put(gate_scr, js[blk], jnp.sum(raw[blk], axis=1, keepdims=True))
        which = (js[blk] == nblk - 1).astype(jnp.int32)
        weights.append(softmax_partial(js[blk], raw[blk] + bias_ref[which]))
    for blk in blocks:
        acc_scr[js[blk]] = sum(_dot(weights[blk][:, pg * PAGE_SIZE:(pg + 1) * PAGE_SIZE],
                                    page_view(v_refs[blk * PAGES_PER_BLOCK + pg]), NT)
                               for pg in range(PAGES_PER_BLOCK))

    @pl.when(step == pl.num_programs(1) - 1)
    def _():
        knew = knew_ref[0].astype(BF16)
        vnew = vnew_ref[0].astype(BF16)
        acc_scr[nblk] = _dot(softmax_partial(nblk, _dot(qbd, knew, NT) + bown_ref[...]), vnew)
        gates = gate_scr[...]
        valid = col < nblk
        rank = jnp.zeros((DEC_ROWS, LANES), jnp.int32)
        for jp in range(nblk):
            gj = gates[:, jp:jp + 1]
            rank = rank + ((gj > gates) | ((gj == gates) & (jp < col))).astype(jnp.int32)
        sel = (valid & (rank < MOBA_TOPK)) | (col == nblk)
        m_all = m_scr[...]
        m_top = jnp.max(jnp.where(sel, m_all, NEG_BIG), axis=1, keepdims=True)
        wgt = jnp.where(sel, jnp.exp2(m_all - m_top), 0.0)
        denom = jnp.sum(wgt * l_scr[...], axis=1, keepdims=True)
        total = jnp.zeros((DEC_ROWS, AT_WIDTH), F32)
        for jj in range(nblk + 1):
            total = total + wgt[:, jj:jj + 1] * acc_scr[jj]
        out = jnp.where(own_head, total / denom, 0.0)
        o_ref[0] = jnp.sum(out.reshape(DEC_Q, AT_HEADS, AT_WIDTH), axis=1)


def _moba_decode(page_table, qb, k_new, v_new, cache_k, cache_v, bias2, bown):
    bsz = qb.shape[0]
    n_pages = page_table.shape[1]
    nblk = n_pages // PAGES_PER_BLOCK
    assert nblk % DEC_BLOCKS_PER_STEP == 0 and nblk + 1 <= LANES
    kt = jnp.transpose(cache_k, (0, 2, 3, 1))
    vt = jnp.transpose(cache_v, (0, 2, 3, 1))
    page = lambda off: pl.BlockSpec((1, AT_HEADS, HEAD_DIM, PAGE_SIZE),
                                    lambda b, s, pt: (pt[b, DEC_PAGES_PER_STEP * s + off], 0, 0, 0))
    per_b = lambda rows: pl.BlockSpec((1, rows, AT_WIDTH), lambda b, s, pt: (b, 0, 0))
    full = lambda a: pl.BlockSpec(a.shape, lambda b, s, pt: (0,) * a.ndim)
    pages = [page(off) for off in range(DEC_PAGES_PER_STEP)]
    kern = functools.partial(_moba_decode_kernel, nblk=nblk)
    grid_spec = pltpu.PrefetchScalarGridSpec(
        num_scalar_prefetch=1,
        grid=(bsz, nblk // DEC_BLOCKS_PER_STEP),
        in_specs=[per_b(DEC_Q), per_b(DEC_PAD), per_b(DEC_PAD)] + pages + pages + [full(bias2), full(bown)],
        out_specs=per_b(DEC_Q),
        scratch_shapes=[pltpu.VMEM((DEC_ROWS, AT_WIDTH), BF16),
                        pltpu.VMEM((DEC_ROWS, LANES), F32),
                        pltpu.VMEM((DEC_ROWS, LANES), F32),
                        pltpu.VMEM((DEC_ROWS, LANES), F32),
                        pltpu.VMEM((nblk + 1, DEC_ROWS, AT_WIDTH), F32)],
    )
    return pl.pallas_call(
        kern,
        grid_spec=grid_spec,
        out_shape=jax.ShapeDtypeStruct((bsz, DEC_Q, AT_WIDTH), F32),
        compiler_params=pltpu.CompilerParams(dimension_semantics=("parallel", "arbitrary"),
                                             vmem_limit_bytes=VMEM_LIMIT_V7X),
        name="moba_decode",
    )(page_table, qb, k_new, v_new, *([kt] * DEC_PAGES_PER_STEP), *([vt] * DEC_PAGES_PER_STEP), bias2, bown)


def _decode_bias(bias):
    last = jnp.transpose(bias[:, 1, :, :DEC_Q], (2, 0, 1)).reshape(DEC_ROWS, MOBA_BLOCK)
    far = jnp.broadcast_to(bias[:, 2, 0, 0][None, :, None], (DEC_Q, AT_HEADS, MOBA_BLOCK)).reshape(DEC_ROWS, MOBA_BLOCK)
    own = jnp.transpose(bias[:, 0, :DEC_PAD, :DEC_Q], (2, 0, 1)).reshape(DEC_ROWS, DEC_PAD)
    return jnp.stack([far, last]), own


def _out_proj_kernel(x_ref, yrw_ref, grw_ref, yat_ref, gat_ref, m_ref, worw_ref, woat_ref, wout_ref, o_ref):
    o_rw = _dot((yrw_ref[...] * _silu(grw_ref[...].astype(F32))).astype(BF16), worw_ref[...])
    o_at = _dot((yat_ref[...] * _silu(gat_ref[...].astype(F32))).astype(BF16), woat_ref[...])
    m = m_ref[...].astype(F32)
    merged = _sigmoid(m[:, :D_MODEL]) * o_rw + _sigmoid(m[:, D_MODEL:]) * o_at
    o_ref[...] = x_ref[...] + _dot(merged.astype(BF16), wout_ref[...])


def _out_proj(x2d, y_rw, g_rw, y_at, g_at, m, w_o_rwkv, w_o_attn, w_out, tm):
    n = x2d.shape[0]
    row = lambda w: pl.BlockSpec((tm, w), lambda i: (i, 0))
    full = lambda a: pl.BlockSpec(a.shape, lambda i: (0,) * a.ndim)
    ws = [w_o_rwkv.astype(BF16), w_o_attn.astype(BF16), w_out.astype(BF16)]
    return pl.pallas_call(
        _out_proj_kernel,
        grid=(n // tm,),
        in_specs=[row(D_MODEL), row(RW_WIDTH), row(RW_WIDTH), row(AT_WIDTH), row(AT_WIDTH), row(2 * D_MODEL)]
                 + [full(w) for w in ws],
        out_specs=row(D_MODEL),
        out_shape=jax.ShapeDtypeStruct((n, D_MODEL), F32),
        compiler_params=pltpu.CompilerParams(dimension_semantics=("parallel",),
                                             vmem_limit_bytes=VMEM_LIMIT_V7X),
        name="out_proj",
    )(x2d, y_rw, g_rw, y_at, g_at, m, *ws)


IN_PROJ_ROWS = 512


def _row_tile(n):
    return 256 if n % 256 == 0 else n


def kernel(x_prompt, x_sample, cache_k, cache_v, page_table, state_shift, state_wkv, rel_table, norm_g, w_in, rw_mu, rw_w0, rw_w_up, rw_a0, rw_a_up, rw_k_k, rw_k_a, rw_r_k, rw_gn_w, rw_gn_b, at_q_norm, at_k_norm, w_o_rwkv, w_o_attn, w_out):
    depth = w_in.shape[0]
    assert depth == 1
    l = 0
    bp, tp, _ = x_prompt.shape
    bs, ts, _ = x_sample.shape
    assert ts == DEC_Q
    rw_params = (rw_mu[l], rw_w0[l], rw_w_up[l], rw_a0[l], rw_a_up[l], rw_k_k[l], rw_k_a[l], rw_r_k[l],
                 rw_gn_w[l], rw_gn_b[l])
    w_in_b = w_in[l].astype(BF16)
    bias = _bias_tables(rel_table)

    xp = x_prompt.reshape(bp * tp, D_MODEL)
    zrw, grw, gat, m, kt, vt, qb, kb, vtb = _in_proj(xp, norm_g[l], w_in_b, at_q_norm[l], at_k_norm[l],
                                                     IN_PROJ_ROWS, seq_len=tp)
    zrw3 = zrw.reshape(bp, tp, RW_SHIFT_WIDTH)
    y_rw, wkv_p = _rwkv(zrw3, jnp.zeros((bp, RW_SHIFT_WIDTH), F32),
                        jnp.zeros((bp, RW_HEADS, HEAD_DIM, HEAD_DIM), F32), rw_params, 64, 64)
    y_at = _moba_prompt(qb.reshape(bp, tp, AT_WIDTH), kb.reshape(bp, tp, AT_WIDTH), vtb, bias)
    y_p = _out_proj(xp, y_rw.reshape(bp * tp, RW_WIDTH), grw, y_at.reshape(bp * tp, AT_WIDTH), gat, m,
                    w_o_rwkv[l], w_o_attn[l], w_out[l], _row_tile(bp * tp)).reshape(bp, tp, D_MODEL)
    k_p = jnp.transpose(kt.reshape(bp, AT_HEADS, HEAD_DIM, tp), (0, 3, 1, 2))[None]
    v_p = jnp.transpose(vt.reshape(bp, AT_HEADS, HEAD_DIM, tp), (0, 3, 1, 2))[None]
    shift_p = zrw3[:, tp - 1][None]

    xs = x_sample.reshape(bs * ts, D_MODEL)
    zrw, grw, gat, m, k, v, qb, kb, vb = _in_proj(xs, norm_g[l], w_in_b, at_q_norm[l], at_k_norm[l], _row_tile(bs * ts))
    zrw3 = zrw.reshape(bs, ts, RW_SHIFT_WIDTH)
    zpad = jnp.pad(zrw3, ((0, 0), (0, DEC_PAD - ts), (0, 0)))
    y_rw, wkv_s = _rwkv(zpad, state_shift[l], state_wkv[l], rw_params, DEC_PAD, ts)
    y_rw = y_rw[:, :ts]
    pad_rows = lambda a: jnp.pad(a.reshape(bs, ts, AT_WIDTH), ((0, 0), (0, DEC_PAD - ts), (0, 0)))
    bias2, bown = _decode_bias(bias)
    y_at = _moba_decode(page_table, qb.reshape(bs, ts, AT_WIDTH), pad_rows(k), pad_rows(v),
                        cache_k[l], cache_v[l], bias2, bown)
    y_s = _out_proj(xs, y_rw.reshape(bs * ts, RW_WIDTH), grw, y_at.reshape(bs * ts, AT_WIDTH), gat, m,
                    w_o_rwkv[l], w_o_attn[l], w_out[l], _row_tile(bs * ts)).reshape(bs, ts, D_MODEL)
    k_s = k.reshape(1, bs, ts, AT_HEADS, HEAD_DIM)
    v_s = v.reshape(1, bs, ts, AT_HEADS, HEAD_DIM)
    shift_s = zrw3[:, ts - 1][None]

    return (y_p, y_s, k_p, v_p, shift_p, wkv_p[None], k_s, v_s, shift_s, wkv_s[None])
```
